```python
import math, functools
import jax, jax.numpy as jnp
from jax import lax
import numpy as np

D_MODEL = 1024
BATCH = 1
SEQ = 16384
DEPTH = 1
DEC_BATCH = 128
DEC_SEQ = 8
PAST_LEN = 8192
PAGE_SIZE = 128

MIX_WIDTH = D_MODEL
DN_HEADS = 4
DN_HEAD_DIM = MIX_WIDTH // 2 // DN_HEADS
DN_WIDTH = DN_HEADS * DN_HEAD_DIM
DN_CONV_W = 4
DN_CONV_DIM = 3 * DN_WIDTH
DN_CHUNK = 64
SB_HEADS = 8
SB_HEAD_DIM = (MIX_WIDTH - DN_WIDTH) // SB_HEADS
SB_WIDTH = SB_HEADS * SB_HEAD_DIM
SB_BLOCK = 128
SB_BIAS_LO = -8.0
SB_BIAS_HI = -6.0
D_FF = ((8 * D_MODEL // 3 + 255) // 256) * 256
N_MOD = 9
IN_COLS = DN_CONV_DIM + DN_WIDTH + 2 * DN_HEADS + 3 * SB_WIDTH
NORM_EPS = 1e-6
L2_EPS = 1e-6

kernel_name = "hymba_gdn_stickbreak_macaron_step"


def rmsnorm(x, gain):
    xf = x.astype(jnp.float32)
    y = xf * lax.rsqrt(jnp.mean(xf * xf, axis=-1, keepdims=True) + NORM_EPS)
    return (y * gain.astype(jnp.float32)).astype(x.dtype)


def l2norm(x):
    return x * lax.rsqrt(jnp.sum(x * x, axis=-1, keepdims=True) + L2_EPS)


def adaln(c, w_ada, b_ada):
    m = jax.nn.silu(c) @ w_ada + b_ada
    return m.reshape(c.shape[0], N_MOD, 1, D_MODEL)


def modulate(x, gain, shift, scale):
    return rmsnorm(x, gain) * (1.0 + scale) + shift


def swiglu(h, w_gate, w_up, w_down):
    return (jax.nn.silu(h @ w_gate) * (h @ w_up)) @ w_down


def causal_conv(x_new, buf, w):
    t = x_new.shape[1]
    xp = jnp.concatenate([buf.astype(x_new.dtype), x_new], axis=1)
    y = xp[:, 0:t] * w[0]
    for i in range(1, DN_CONV_W):
        y = y + xp[:, i:i + t] * w[i]
    return jax.nn.silu(y), xp[:, xp.shape[1] - (DN_CONV_W - 1):]


def _to_chunks(x, n, c):
    x = x.reshape((x.shape[0], n, c) + x.shape[2:])
    return jnp.moveaxis(x, 3, 1)


def gated_delta_rule(q, k, v, g, beta, s0):
    bsz, t, h, _ = q.shape
    dv = v.shape[-1]
    c = min(DN_CHUNK, t)
    n = -(-t // c)
    pad = n * c - t

    def prep(x):
        x = jnp.pad(x, [(0, 0), (0, pad)] + [(0, 0)] * (x.ndim - 2))
        return _to_chunks(x, n, c)

    qc, kc, vc, gc, bc = [prep(a) for a in (q, k, v, g, beta)]
    gam = jnp.cumsum(gc, axis=-1)
    idx = jnp.arange(c)
    lower_incl = idx[:, None] >= idx[None, :]
    strict = idx[:, None] > idx[None, :]
    diff = gam[..., :, None] - gam[..., None, :]
    decay = jnp.where(lower_incl, jnp.exp(jnp.where(lower_incl, diff, 0.0)), 0.0)
    kb = kc * bc[..., None]
    a_mat = jnp.where(strict, jnp.einsum('bhnid,bhnjd->bhnij', kb, kc) * decay, 0.0)
    m = a_mat + jnp.eye(c, dtype=jnp.float32)
    solve = functools.partial(lax.linalg.triangular_solve, left_side=True, lower=True, unit_diagonal=True)
    u = solve(m, vc * bc[..., None])
    w = solve(m, kb * jnp.exp(gam)[..., None])
    attn = jnp.einsum('bhnid,bhnjd->bhnij', qc, kc) * decay
    q_dec = qc * jnp.exp(gam)[..., None]
    gam_last = gam[..., -1]
    k_dec = kc * jnp.exp(gam_last[..., None] - gam)[..., None]

    def step(s, xs):
        u_c, w_c, attn_c, qd_c, kd_c, gl_c = xs
        v_new = u_c - jnp.einsum('bhcd,bhde->bhce', w_c, s)
        o_c = jnp.einsum('bhcd,bhde->bhce', qd_c, s) + jnp.einsum('bhij,bhje->bhie', attn_c, v_new)
        s = s * jnp.exp(gl_c)[..., None, None] + jnp.einsum('bhcd,bhce->bhde', kd_c, v_new)
        return s, o_c

    xs = tuple(jnp.moveaxis(a, 2, 0) for a in (u, w, attn, q_dec, k_dec, gam_last))
    s_final, o = lax.scan(step, s0, xs)
    o = jnp.moveaxis(jnp.moveaxis(o, 0, 2), 1, 3).reshape(bsz, n * c, h, dv)[:, :t]
    return o, s_final


def deltanet_mixer(qkv_raw, z, a, b, conv_prev, ssm_prev, conv_w, a_log, dt_bias, norm_g):
    bsz, t, _ = qkv_raw.shape
    qkv, conv_new = causal_conv(qkv_raw, conv_prev, conv_w)
    qkv = qkv.astype(jnp.float32).reshape(bsz, t, 3, DN_HEADS, DN_HEAD_DIM)
    q = l2norm(qkv[:, :, 0]) * (DN_HEAD_DIM ** -0.5)
    k = l2norm(qkv[:, :, 1])
    v = qkv[:, :, 2]
    g = -jnp.exp(a_log.astype(jnp.float32)) * jax.nn.softplus(a.astype(jnp.float32) + dt_bias.astype(jnp.float32))
    beta = jax.nn.sigmoid(b.astype(jnp.float32))
    o, ssm_new = gated_delta_rule(q, k, v, g, beta, ssm_prev.astype(jnp.float32))
    o = rmsnorm(o, norm_g) * jax.nn.silu(z.astype(jnp.float32).reshape(bsz, t, DN_HEADS, DN_HEAD_DIM))
    return o.reshape(bsz, t, DN_WIDTH).astype(qkv_raw.dtype), conv_new, ssm_new.astype(ssm_prev.dtype)


def stick_breaking(q, k_segs, v_segs, q_pos, k_pos, bias):
    z = jnp.concatenate([jnp.einsum('bqhd,bkhd->bhqk', q, kk, preferred_element_type=jnp.float32)
                         for kk in k_segs], axis=-1) * (SB_HEAD_DIM ** -0.5)
    z = z + bias.astype(jnp.float32)[None, :, None, None]
    visible = k_pos[None, :] < q_pos[:, None]
    log_keep = jnp.where(visible, jax.nn.log_sigmoid(-z), 0.0)
    log_pass = lax.cumsum(log_keep, axis=3, reverse=True) - log_keep
    wts = jnp.where(visible, jnp.exp(jax.nn.log_sigmoid(z) + log_pass), 0.0)
    out = None
    off = 0
    for vv in v_segs:
        n = vv.shape[1]
        part = jnp.einsum('bhqk,bkhd->bqhd', wts[..., off:off + n].astype(vv.dtype), vv,
                          preferred_element_type=jnp.float32)
        out = part if out is None else out + part
        off += n
    return out.astype(q.dtype)


def sb_prompt_attend(q, k, v, bias):
    bsz, t, h, d = q.shape
    nb = t // SB_BLOCK
    pos = jnp.arange(t, dtype=jnp.int32)
    qb = jnp.moveaxis(q.reshape(bsz, nb, SB_BLOCK, h, d), 1, 0)
    pb = pos.reshape(nb, SB_BLOCK)
    ob = lax.map(lambda a: stick_breaking(a[0], [k], [v], a[1], pos, bias), (qb, pb))
    return jnp.moveaxis(ob, 0, 1).reshape(bsz, t, h, d)


def make_sample_attend(past_k, past_v):
    def attend(q, k, v, bias):
        p = past_k.shape[1]
        t = q.shape[1]
        k_pos = jnp.arange(p + t, dtype=jnp.int32)
        q_pos = p + jnp.arange(t, dtype=jnp.int32)
        return stick_breaking(q, [past_k, k], [past_v, v], q_pos, k_pos, bias)
    return attend


def gather_pages(pool, page_table):
    g = pool[page_table]
    return g.reshape((page_table.shape[0], page_table.shape[1] * pool.shape[1]) + pool.shape[2:])


def split_points():
    sizes = [DN_CONV_DIM, DN_WIDTH, DN_HEADS, DN_HEADS, SB_WIDTH, SB_WIDTH]
    pts, acc = [], 0
    for s in sizes:
        acc += s
        pts.append(acc)
    return pts


def trunk_layer(x, c, conv_prev, ssm_prev, attend, p):
    bsz, t, _ = x.shape
    mods = adaln(c, p['w_ada'], p['b_ada'])
    h = modulate(x, p['g_ffn1'], mods[:, 0], mods[:, 1])
    x = x + 0.5 * mods[:, 2] * swiglu(h, p['w1_gate'], p['w1_up'], p['w1_down'])
    h = modulate(x, p['g_mix'], mods[:, 3], mods[:, 4])
    proj = h @ p['w_in']
    dn_qkv, dn_z, dn_a, dn_b, sb_q, sb_k, sb_v = jnp.split(proj, split_points(), axis=-1)
    o_dn, conv_new, ssm_new = deltanet_mixer(dn_qkv, dn_z, dn_a, dn_b, conv_prev, ssm_prev,
                                             p['dn_conv_w'], p['dn_a_log'], p['dn_dt_bias'], p['dn_norm_g'])
    q = sb_q.reshape(bsz, t, SB_HEADS, SB_HEAD_DIM)
    k = sb_k.reshape(bsz, t, SB_HEADS, SB_HEAD_DIM)
    v = sb_v.reshape(bsz, t, SB_HEADS, SB_HEAD_DIM)
    o_sb = rmsnorm(attend(q, k, v, p['sb_bias']), p['sb_norm_g']).reshape(bsz, t, SB_WIDTH)
    mixed = jnp.concatenate([o_dn, o_sb], axis=-1) @ p['w_out']
    x = x + mods[:, 5] * mixed
    h = modulate(x, p['g_ffn2'], mods[:, 6], mods[:, 7])
    x = x + 0.5 * mods[:, 8] * swiglu(h, p['w2_gate'], p['w2_up'], p['w2_down'])
    return x, k, v, conv_new, ssm_new


def setup_inputs(seed: int = 0) -> dict:
    key = jax.random.key(seed)
    ks = jax.random.split(key, 32)
    f32 = jnp.float32

    def nrm(i, shape, scale):
        return jax.random.normal(ks[i], shape, f32) * scale

    n_pages = PAST_LEN // PAGE_SIZE
    n_used = DEC_BATCH * n_pages
    n_pool = n_used + max(1, n_used // 4)
    page_table = jax.random.permutation(ks[0], n_pool)[:n_used].astype(jnp.int32).reshape(DEC_BATCH, n_pages)
    dt = jnp.exp(jax.random.uniform(ks[1], (DEPTH, DN_HEADS), f32) * (math.log(0.1) - math.log(0.001)) + math.log(0.001))
    dn_dt_bias = dt + jnp.log(-jnp.expm1(-dt))
    dn_a_log = jnp.log(jax.random.uniform(ks[2], (DEPTH, DN_HEADS), f32, 1.0, 16.0))
    sb_bias = jax.random.uniform(ks[28], (DEPTH, SB_HEADS), f32, SB_BIAS_LO, SB_BIAS_HI)
    return {
        'x_prompt': nrm(3, (BATCH, SEQ, D_MODEL), 1.0),
        'x_sample': nrm(4, (DEC_BATCH, DEC_SEQ, D_MODEL), 1.0),
        'c_prompt': nrm(5, (BATCH, D_MODEL), 1.0),
        'c_sample': nrm(6, (DEC_BATCH, D_MODEL), 1.0),
        'cache_sb_k': nrm(7, (DEPTH, n_pool, PAGE_SIZE, SB_HEADS, SB_HEAD_DIM), 1.0),
        'cache_sb_v': nrm(8, (DEPTH, n_pool, PAGE_SIZE, SB_HEADS, SB_HEAD_DIM), 1.0),
        'page_table': page_table,
        'state_dn_ssm': nrm(9, (DEPTH, DEC_BATCH, DN_HEADS, DN_HEAD_DIM, DN_HEAD_DIM), DN_HEAD_DIM ** -0.5),
        'state_dn_conv': nrm(10, (DEPTH, DEC_BATCH, DN_CONV_W - 1, DN_CONV_DIM), 1.0),
        'w_ada': nrm(11, (DEPTH, D_MODEL, N_MOD * D_MODEL), 0.5 * D_MODEL ** -0.5),
        'b_ada': nrm(12, (DEPTH, N_MOD * D_MODEL), 0.01),
        'g_ffn1': 1.0 + nrm(13, (DEPTH, D_MODEL), 0.02),
        'w1_gate': nrm(14, (DEPTH, D_MODEL, D_FF), D_MODEL ** -0.5),
        'w1_up': nrm(15, (DEPTH, D_MODEL, D_FF), D_MODEL ** -0.5),
        'w1_down': nrm(16, (DEPTH, D_FF, D_MODEL), D_FF ** -0.5),
        'g_mix': 1.0 + nrm(17, (DEPTH, D_MODEL), 0.02),
        'w_in': nrm(18, (DEPTH, D_MODEL, IN_COLS), D_MODEL ** -0.5),
        'dn_conv_w': nrm(19, (DEPTH, DN_CONV_W, DN_CONV_DIM), DN_CONV_W ** -0.5),
        'dn_a_log': dn_a_log,
        'dn_dt_bias': dn_dt_bias,
        'dn_norm_g': 1.0 + nrm(20, (DEPTH, DN_HEAD_DIM), 0.02),
        'sb_norm_g': 1.0 + nrm(21, (DEPTH, SB_HEAD_DIM), 0.02),
        'sb_bias': sb_bias,
        'w_out': nrm(22, (DEPTH, MIX_WIDTH, D_MODEL), MIX_WIDTH ** -0.5),
        'g_ffn2': 1.0 + nrm(23, (DEPTH, D_MODEL), 0.02),
        'w2_gate': nrm(24, (DEPTH, D_MODEL, D_FF), D_MODEL ** -0.5),
        'w2_up': nrm(25, (DEPTH, D_MODEL, D_FF), D_MODEL ** -0.5),
        'w2_down': nrm(26, (DEPTH, D_FF, D_MODEL), D_FF ** -0.5),
        'g_final': 1.0 + nrm(27, (D_MODEL,), 0.02),
    }


def reference(x_prompt, x_sample, c_prompt, c_sample, cache_sb_k, cache_sb_v, page_table,
              state_dn_ssm, state_dn_conv, w_ada, b_ada, g_ffn1, w1_gate, w1_up, w1_down,
              g_mix, w_in, dn_conv_w, dn_a_log, dn_dt_bias, dn_norm_g, sb_norm_g, sb_bias, w_out,
              g_ffn2, w2_gate, w2_up, w2_down, g_final):
    xp, xs = x_prompt, x_sample
    bsz = x_prompt.shape[0]
    kp_l, vp_l, ssmp_l, convp_l = [], [], [], []
    ks_l, vs_l, ssms_l, convs_l = [], [], [], []
    for l in range(DEPTH):
        p = dict(w_ada=w_ada[l], b_ada=b_ada[l], g_ffn1=g_ffn1[l], w1_gate=w1_gate[l], w1_up=w1_up[l],
                 w1_down=w1_down[l], g_mix=g_mix[l], w_in=w_in[l], dn_conv_w=dn_conv_w[l],
                 dn_a_log=dn_a_log[l], dn_dt_bias=dn_dt_bias[l], dn_norm_g=dn_norm_g[l],
                 sb_norm_g=sb_norm_g[l], sb_bias=sb_bias[l], w_out=w_out[l], g_ffn2=g_ffn2[l],
                 w2_gate=w2_gate[l], w2_up=w2_up[l], w2_down=w2_down[l])
        conv0 = jnp.zeros((bsz, DN_CONV_W - 1, DN_CONV_DIM), state_dn_conv.dtype)
        ssm0 = jnp.zeros((bsz, DN_HEADS, DN_HEAD_DIM, DN_HEAD_DIM), state_dn_ssm.dtype)
        xp, kp, vp, convp, ssmp = trunk_layer(xp, c_prompt, conv0, ssm0, sb_prompt_attend, p)
        past_k = gather_pages(cache_sb_k[l], page_table)
        past_v = gather_pages(cache_sb_v[l], page_table)
        xs, ks_, vs_, convs, ssms = trunk_layer(xs, c_sample, state_dn_conv[l], state_dn_ssm[l],
                                                make_sample_attend(past_k, past_v), p)
        kp_l.append(kp); vp_l.append(vp); ssmp_l.append(ssmp); convp_l.append(convp)
        ks_l.append(ks_); vs_l.append(vs_); ssms_l.append(ssms); convs_l.append(convs)
    y_prompt = rmsnorm(xp, g_final)
    y_sample = rmsnorm(xs, g_final)
    return (y_prompt, y_sample,
            jnp.stack(kp_l), jnp.stack(vp_l), jnp.stack(ssmp_l), jnp.stack(convp_l),
            jnp.stack(ks_l), jnp.stack(vs_l), jnp.stack(ssms_l), jnp.stack(convs_l))
```

```python
import functools

import jax
import jax.numpy as jnp
from jax import lax
from jax.experimental import pallas as pl
from jax.experimental.pallas import tpu as pltpu

F32 = jnp.float32
BF16 = jnp.bfloat16

DN_HEADS = 4
DN_HEAD_DIM = 128
DN_WIDTH = DN_HEADS * DN_HEAD_DIM
DN_CONV_W = 4
DN_CONV_DIM = 3 * DN_WIDTH
DN_CHUNK = 64
SB_HEADS = 8
SB_HEAD_DIM = 64
SB_WIDTH = SB_HEADS * SB_HEAD_DIM
N_MOD = 9
NORM_EPS = 1e-6
L2_EPS = 1e-6
LANES = 128

NN = (((1,), (0,)), ((), ()))
NT = (((1,), (1,)), ((), ()))
TN = (((0,), (0,)), ((), ()))

VMEM_LIMIT = 56 * 1024 * 1024


def _params(sem):
    return pltpu.CompilerParams(dimension_semantics=sem, vmem_limit_bytes=VMEM_LIMIT)


def _dot(a, b, dims=NN):
    return lax.dot_general(a, b, dims, preferred_element_type=F32)


def _split2(x):
    hi = x.astype(BF16)
    lo = (x - hi.astype(F32)).astype(BF16)
    return hi, lo


def _split3(x):
    hi = x.astype(BF16)
    r = x - hi.astype(F32)
    mid = r.astype(BF16)
    lo = (r - mid.astype(F32)).astype(BF16)
    return hi, mid, lo


def _mm3(a, b, dims=NN):
    ah, al = _split2(a)
    bh, bl = _split2(b)
    return _dot(ah, bh, dims) + (_dot(ah, bl, dims) + _dot(al, bh, dims))


def _mm_mask_left(m01, x):
    hi, mid, lo = _split3(x)
    return _dot(m01, hi) + (_dot(m01, mid) + _dot(m01, lo))


def _sigmoid(x):
    return 1.0 / (1.0 + jnp.exp(-x))


def _silu(x):
    return x * _sigmoid(x)


def _softplus(x):
    return jnp.maximum(x, 0.0) + jnp.log1p(jnp.exp(-jnp.abs(x)))


def _modnorm(x, gain, shift, scale):
    y = x * lax.rsqrt(jnp.mean(x * x, axis=-1, keepdims=True) + NORM_EPS)
    return (y * gain) * (1.0 + scale) + shift


def _adaln_kernel(c_ref, w_ref, b_ref, o_ref):
    c = _silu(c_ref[...]).astype(BF16)
    o_ref[...] = _dot(c, w_ref[...]) + b_ref[...]


def _adaln(c, w_bf, b):
    n, d = c.shape
    cols = w_bf.shape[1]
    tn = d
    return pl.pallas_call(
        _adaln_kernel,
        grid=(cols // tn,),
        in_specs=[pl.BlockSpec((n, d), lambda j: (0, 0)),
                  pl.BlockSpec((d, tn), lambda j: (0, j)),
                  pl.BlockSpec((1, tn), lambda j: (0, j))],
        out_specs=pl.BlockSpec((n, tn), lambda j: (0, j)),
        out_shape=jax.ShapeDtypeStruct((n, cols), F32),
        compiler_params=_params(("arbitrary",)),
        name="adaln",
    )(c, w_bf, b.reshape(1, cols))


def _row_blocks(g, r, target):
    if r >= target:
        assert r % target == 0
        return 1, target
    gb = min(g, target // r)
    assert g % gb == 0
    return gb, r


def _ffn_kernel(x_ref, sh_ref, sc_ref, gt_ref, gain_ref, wg_ref, wu_ref, wd_ref, *rest, final_norm):
    if final_norm:
        gfin_ref, o_ref, h_scr, acc_scr = rest
    else:
        o_ref, h_scr, acc_scr = rest
    f = pl.program_id(2)
    rows, d = h_scr.shape

    @pl.when(f == 0)
    def _():
        h = _modnorm(x_ref[...], gain_ref[...], sh_ref[...], sc_ref[...])
        h_scr[...] = h.reshape(rows, d).astype(BF16)
        acc_scr[...] = jnp.zeros_like(acc_scr)

    h = h_scr[...]
    a = _dot(h, wg_ref[...])
    b = _dot(h, wu_ref[...])
    act = (_silu(a) * b).astype(BF16)
    acc_scr[...] += _dot(act, wd_ref[...])

    @pl.when(f == pl.num_programs(2) - 1)
    def _():
        x = x_ref[...]
        y = x + 0.5 * gt_ref[...] * acc_scr[...].reshape(x.shape)
        if final_norm:
            y = y * lax.rsqrt(jnp.mean(y * y, axis=-1, keepdims=True) + NORM_EPS) * gfin_ref[...]
        o_ref[...] = y


def _ffn(x, shift, scale, gate, gain, wg, wu, wd, gfin=None, rows_target=512, tf=256):
    g, r, d = x.shape
    dff = wg.shape[1]
    gb, rb = _row_blocks(g, r, rows_target)
    rows = gb * rb
    mod_spec = pl.BlockSpec((gb, 1, d), lambda i, j, f: (i, 0, 0))
    vec_spec = pl.BlockSpec((1, d), lambda i, j, f: (0, 0))
    in_specs = [pl.BlockSpec((gb, rb, d), lambda i, j, f: (i, j, 0)),
                mod_spec, mod_spec, mod_spec, vec_spec,
                pl.BlockSpec((d, tf), lambda i, j, f: (0, f)),
                pl.BlockSpec((d, tf), lambda i, j, f: (0, f)),
                pl.BlockSpec((tf, d), lambda i, j, f: (f, 0))]
    args = [x, shift, scale, gate, gain.reshape(1, d), wg, wu, wd]
    if gfin is not None:
        in_specs.append(vec_spec)
        args.append(gfin.reshape(1, d))
    return pl.pallas_call(
        functools.partial(_ffn_kernel, final_norm=gfin is not None),
        grid=(g // gb, r // rb, dff // tf),
        in_specs=in_specs,
        out_specs=pl.BlockSpec((gb, rb, d), lambda i, j, f: (i, j, 0)),
        out_shape=jax.ShapeDtypeStruct(x.shape, F32),
        scratch_shapes=[pltpu.VMEM((rows, d), BF16), pltpu.VMEM((rows, d), F32)],
        compiler_params=_params(("arbitrary", "arbitrary", "arbitrary")),
        name="ffn",
    )(*args)


PROJ_COLS = (DN_CONV_DIM, DN_WIDTH, SB_WIDTH, LANES)


def _proj_kernel(x_ref, sh_ref, sc_ref, gain_ref, w_ref, wk_ref, wv_ref, *o_refs, transposed):
    rows = o_refs[0].shape[0]
    h = _modnorm(x_ref[...], gain_ref[...], sh_ref[...], sc_ref[...])
    h = h.reshape(rows, h.shape[-1]).astype(BF16)
    off = 0
    for o_ref, cols in zip(o_refs[0:len(PROJ_COLS)], PROJ_COLS):
        o_ref[...] = _dot(h, w_ref[:, off:off + cols])
        off += cols
    if transposed:
        k_ref, v_ref, kb_ref, vb_ref = o_refs[len(PROJ_COLS):]
        kt = _dot(wk_ref[...], h, NT)
        vt = _dot(wv_ref[...], h, NT)
        k_ref[...] = kt
        v_ref[...] = vt
        kb_ref[...] = kt.astype(BF16)
        vb_ref[...] = vt.astype(BF16)
    else:
        k_ref, v_ref = o_refs[len(PROJ_COLS):]
        k_ref[...] = _dot(h, wk_ref[...])
        v_ref[...] = _dot(h, wv_ref[...])


def _proj(x, shift, scale, gain, w_r, w_k, w_v, transposed, rows_target=512):
    g, r, d = x.shape
    gb, rb = _row_blocks(g, r, rows_target)
    rows = gb * rb
    nj = r // rb
    n = g * r
    mod_spec = pl.BlockSpec((gb, 1, d), lambda i, j: (i, 0, 0))
    const = lambda a: pl.BlockSpec(a.shape, lambda i, j: (0, 0))
    out_specs = [pl.BlockSpec((rows, c), lambda i, j: (i * nj + j, 0)) for c in PROJ_COLS]
    out_shape = [jax.ShapeDtypeStruct((n, c), F32) for c in PROJ_COLS]
    if transposed:
        out_specs += [pl.BlockSpec((SB_WIDTH, rows), lambda i, j: (0, i * nj + j))] * 4
        out_shape += [jax.ShapeDtypeStruct((SB_WIDTH, n), dt) for dt in (F32, F32, BF16, BF16)]
    else:
        out_specs += [pl.BlockSpec((rows, SB_WIDTH), lambda i, j: (i * nj + j, 0))] * 2
        out_shape += [jax.ShapeDtypeStruct((n, SB_WIDTH), F32)] * 2
    return pl.pallas_call(
        functools.partial(_proj_kernel, transposed=transposed),
        grid=(g // gb, nj),
        in_specs=[pl.BlockSpec((gb, rb, d), lambda i, j: (i, j, 0)),
                  mod_spec, mod_spec,
                  pl.BlockSpec((1, d), lambda i, j: (0, 0)),
                  const(w_r), const(w_k), const(w_v)],
        out_specs=out_specs,
        out_shape=out_shape,
        compiler_params=_params(("arbitrary", "arbitrary")),
        name="proj_in",
    )(x, shift, scale, gain.reshape(1, d), w_r, w_k, w_v)


def _outproj_kernel(x_ref, gt_ref, odn_ref, osb_ref, w_ref, o_ref):
    x = x_ref[...]
    mixed = (_dot(odn_ref[...].astype(BF16), w_ref[0:DN_WIDTH, :])
             + _dot(osb_ref[...].astype(BF16), w_ref[DN_WIDTH:DN_WIDTH + SB_WIDTH, :]))
    o_ref[...] = x + gt_ref[...] * mixed.reshape(x.shape)


def _outproj(x, gate, o_dn, o_sb, w_out, rows_target=512):
    g, r, d = x.shape
    gb, rb = _row_blocks(g, r, rows_target)
    rows = gb * rb
    nj = r // rb
    return pl.pallas_call(
        _outproj_kernel,
        grid=(g // gb, nj),
        in_specs=[pl.BlockSpec((gb, rb, d), lambda i, j: (i, j, 0)),
                  pl.BlockSpec((gb, 1, d), lambda i, j: (i, 0, 0)),
                  pl.BlockSpec((rows, DN_WIDTH), lambda i, j: (i * nj + j, 0)),
                  pl.BlockSpec((rows, SB_WIDTH), lambda i, j: (i * nj + j, 0)),
                  pl.BlockSpec(w_out.shape, lambda i, j: (0, 0))],
        out_specs=pl.BlockSpec((gb, rb, d), lambda i, j: (i, j, 0)),
        out_shape=jax.ShapeDtypeStruct(x.shape, F32),
        compiler_params=_params(("arbitrary", "arbitrary")),
        name="proj_out",
    )(x, gate, o_dn, o_sb, w_out)


def _dn_masks(group):
    c = DN_CHUNK
    i = jnp.arange(c)[:, None]
    m = jnp.arange(c)[None, :]
    same = (i // group) == (m // group)
    lower = jnp.logical_and(m <= i, same)
    lt = jnp.concatenate([lower, same], axis=0).astype(BF16)
    sext = jnp.concatenate([(i > m), jnp.ones((c, c), bool)], axis=1).astype(F32)
    return lt, sext


def _dn_gates(ab, alog, dtb):
    g = -jnp.exp(alog) * _softplus(ab + dtb)
    beta = _sigmoid(ab)
    return g, beta


def _dn_local(q_raw, k_raw, v, g_col, beta_col, lt, sext, group):
    c = DN_CHUNK
    q = q_raw * lax.rsqrt(jnp.sum(q_raw * q_raw, axis=-1, keepdims=True) + L2_EPS) * (DN_HEAD_DIM ** -0.5)
    k = k_raw * lax.rsqrt(jnp.sum(k_raw * k_raw, axis=-1, keepdims=True) + L2_EPS)
    row = lax.broadcasted_iota(jnp.int32, (c, c), 0)
    col = lax.broadcasted_iota(jnp.int32, (c, c), 1)
    lower = col <= row
    strict = col < row
    if group < c:
        shift = group.bit_length() - 1
        same = (row >> shift) == (col >> shift)
        lower = jnp.logical_and(lower, same)
        strict = jnp.logical_and(strict, same)
    r = _mm_mask_left(lt, g_col * sext)
    diff = r[0:c, 0:c]
    gam = r[0:c, c:c + 1]
    tot = r[c:2 * c, c:c + 1]
    dm = jnp.where(lower, jnp.exp(jnp.where(lower, diff, 0.0)), 0.0)
    kb = k * beta_col
    a = jnp.where(strict, _mm3(kb, k, NT) * dm, 0.0)
    eye = (row == col).astype(F32)
    p = eye - a
    x = a
    n_sq = max(1, (min(group, c) - 1).bit_length() - 1)
    for _ in range(n_sq):
        x = _mm3(x, x)
        p = p + _mm3(p, x)
    eg = jnp.exp(gam)
    uw = _mm3(p, jnp.concatenate([v * beta_col, kb * eg], axis=1))
    u = uw[:, 0:DN_HEAD_DIM]
    w = uw[:, DN_HEAD_DIM:2 * DN_HEAD_DIM]
    attn = _mm3(q, k, NT) * dm
    q_dec = q * eg
    k_dec = k * jnp.exp(tot - gam)
    return u, w, attn, q_dec, k_dec, tot


def _dn_out(o, z, ng):
    y = o * lax.rsqrt(jnp.mean(o * o, axis=-1, keepdims=True) + NORM_EPS) * ng
    return y * _silu(z)


def _dn_prompt_kernel(qkv_ref, z_ref, ab_ref, cw_ref, alog_ref, dtb_ref, ng_ref, lt_ref, sext_ref,
                      o_ref, s_out_ref, conv_out_ref, xp_scr, s_scr):
    c = DN_CHUNK
    dh = DN_HEAD_DIM
    step = pl.program_id(0)

    @pl.when(step == 0)
    def _():
        xp_scr[0:8, :] = jnp.zeros((8, DN_CONV_DIM), F32)
        s_scr[...] = jnp.zeros_like(s_scr)

    x = qkv_ref[...]
    xp_scr[8:8 + c, :] = x
    cw = cw_ref[...]
    y = xp_scr[5:5 + c, :] * cw[0:1, :]
    y = y + xp_scr[6:6 + c, :] * cw[1:2, :]
    y = y + xp_scr[7:7 + c, :] * cw[2:3, :]
    y = y + x * cw[3:4, :]
    conv_out_ref[...] = xp_scr[5 + c:8 + c, :]
    xp_scr[0:8, :] = xp_scr[c:c + 8, :]
    qkv = _silu(y)

    g_all, beta_all = _dn_gates(ab_ref[...], alog_ref[...], dtb_ref[...])
    lt = lt_ref[...]
    sext = sext_ref[...]
    ng = ng_ref[...]
    for h in range(DN_HEADS):
        u, w, attn, q_dec, k_dec, tot = _dn_local(
            qkv[:, h * dh:(h + 1) * dh], qkv[:, DN_WIDTH + h * dh:DN_WIDTH + (h + 1) * dh],
            qkv[:, 2 * DN_WIDTH + h * dh:2 * DN_WIDTH + (h + 1) * dh],
            g_all[:, h:h + 1], beta_all[:, DN_HEADS + h:DN_HEADS + h + 1], lt, sext, c)
        s = s_scr[h]
        v_new = u - _mm3(w, s)
        o = _mm3(q_dec, s) + _mm3(attn, v_new)
        s_new = s * jnp.exp(tot[0:1, :]) + _mm3(k_dec, v_new, TN)
        s_scr[h] = s_new
        s_out_ref[h] = s_new
        o_ref[:, h * dh:(h + 1) * dh] = _dn_out(o, z_ref[:, h * dh:(h + 1) * dh], ng)


def _dn_prompt(qkv, z, ab, conv_w, alog, dtb, ng):
    t = qkv.shape[0]
    c = DN_CHUNK
    lt, sext = _dn_masks(c)
    const = lambda shape: pl.BlockSpec(shape, lambda i: tuple(0 for _ in shape))
    return pl.pallas_call(
        _dn_prompt_kernel,
        grid=(t // c,),
        in_specs=[pl.BlockSpec((c, DN_CONV_DIM), lambda i: (i, 0)),
                  pl.BlockSpec((c, DN_WIDTH), lambda i: (i, 0)),
                  pl.BlockSpec((c, LANES), lambda i: (i, 0)),
                  const((DN_CONV_W, DN_CONV_DIM)), const((1, LANES)), const((1, LANES)),
                  const((1, DN_HEAD_DIM)), const(lt.shape), const(sext.shape)],
        out_specs=[pl.BlockSpec((c, DN_WIDTH), lambda i: (i, 0)),
                   const((DN_HEADS, DN_HEAD_DIM, DN_HEAD_DIM)),
                   const((DN_CONV_W - 1, DN_CONV_DIM))],
        out_shape=[jax.ShapeDtypeStruct((t, DN_WIDTH), F32),
                   jax.ShapeDtypeStruct((DN_HEADS, DN_HEAD_DIM, DN_HEAD_DIM), F32),
                   jax.ShapeDtypeStruct((DN_CONV_W - 1, DN_CONV_DIM), F32)],
        scratch_shapes=[pltpu.VMEM((c + 8, DN_CONV_DIM), F32),
                        pltpu.VMEM((DN_HEADS, DN_HEAD_DIM, DN_HEAD_DIM), F32)],
        compiler_params=_params(("arbitrary",)),
        name="deltanet_prompt",
    )(qkv, z, ab, conv_w, alog, dtb, ng, lt, sext)


def _dn_sample_kernel(qkv_ref, z_ref, ab_ref, cprev_ref, s_ref, cw_ref, alog_ref, dtb_ref, ng_ref,
                      lt_ref, sext_ref, o_ref, s_out_ref, conv_out_ref, xp_scr, *, t):
    c = DN_CHUNK
    dh = DN_HEAD_DIM
    nreq = c // t
    x3 = qkv_ref[...]
    xp_scr[:, 8 - (DN_CONV_W - 1):8, :] = cprev_ref[...]
    xp_scr[:, 8:8 + t, :] = x3
    cw = cw_ref[...]
    y = xp_scr[:, 5:5 + t, :] * cw[0:1, :]
    y = y + xp_scr[:, 6:6 + t, :] * cw[1:2, :]
    y = y + xp_scr[:, 7:7 + t, :] * cw[2:3, :]
    y = y + x3 * cw[3:4, :]
    conv_out_ref[...] = xp_scr[:, 5 + t:8 + t, :]
    qkv = _silu(y).reshape(c, DN_CONV_DIM)

    g_all, beta_all = _dn_gates(ab_ref[...], alog_ref[...], dtb_ref[...])
    lt = lt_ref[...]
    sext = sext_ref[...]
    ng = ng_ref[...]
    rowid = lax.broadcasted_iota(jnp.int32, (c, 1), 0)
    for h in range(DN_HEADS):
        u, w, attn, q_dec, k_dec, tot = _dn_local(
            qkv[:, h * dh:(h + 1) * dh], qkv[:, DN_WIDTH + h * dh:DN_WIDTH + (h + 1) * dh],
            qkv[:, 2 * DN_WIDTH + h * dh:2 * DN_WIDTH + (h + 1) * dh],
            g_all[:, h:h + 1], beta_all[:, DN_HEADS + h:DN_HEADS + h + 1], lt, sext, t)
        ws, qs = [], []
        for r in range(nreq):
            s = s_ref[r, h]
            ws.append(_mm3(w, s)[r * t:(r + 1) * t, :])
            qs.append(_mm3(q_dec, s)[r * t:(r + 1) * t, :])
        v_new = u - jnp.concatenate(ws, axis=0)
        o = jnp.concatenate(qs, axis=0) + _mm3(attn, v_new)
        for r in range(nreq):
            in_req = jnp.logical_and(rowid >= r * t, rowid < (r + 1) * t)
            kd_r = jnp.where(in_req, k_dec, 0.0)
            s_out_ref[r, h] = (s_ref[r, h] * jnp.exp(tot[r * t:r * t + 1, :])
                               + _mm3(kd_r, v_new, TN))
        o_ref[:, h * dh:(h + 1) * dh] = _dn_out(o, z_ref[:, h * dh:(h + 1) * dh], ng)


def _dn_sample(qkv, z, ab, conv_prev, ssm_prev, conv_w, alog, dtb, ng, t):
    n = qkv.shape[0]
    b = n // t
    c = DN_CHUNK
    assert t == 8 and c % t == 0 and b % (c // t) == 0
    nreq = c // t
    lt, sext = _dn_masks(t)
    const = lambda shape: pl.BlockSpec(shape, lambda i: tuple(0 for _ in shape))
    return pl.pallas_call(
        functools.partial(_dn_sample_kernel, t=t),
        grid=(b // nreq,),
        in_specs=[pl.BlockSpec((nreq, t, DN_CONV_DIM), lambda i: (i, 0, 0)),
                  pl.BlockSpec((c, DN_WIDTH), lambda i: (i, 0)),
                  pl.BlockSpec((c, LANES), lambda i: (i, 0)),
                  pl.BlockSpec((nreq, DN_CONV_W - 1, DN_CONV_DIM), lambda i: (i, 0, 0)),
                  pl.BlockSpec((nreq, DN_HEADS, DN_HEAD_DIM, DN_HEAD_DIM), lambda i: (i, 0, 0, 0)),
                  const((DN_CONV_W, DN_CONV_DIM)), const((1, LANES)), const((1, LANES)),
                  const((1, DN_HEAD_DIM)), const(lt.shape), const(sext.shape)],
        out_specs=[pl.BlockSpec((c, DN_WIDTH), lambda i: (i, 0)),
                   pl.BlockSpec((nreq, DN_HEADS, DN_HEAD_DIM, DN_HEAD_DIM), lambda i: (i, 0, 0, 0)),
                   pl.BlockSpec((nreq, DN_CONV_W - 1, DN_CONV_DIM), lambda i: (i, 0, 0))],
        out_shape=[jax.ShapeDtypeStruct((n, DN_WIDTH), F32),
                   jax.ShapeDtypeStruct(ssm_prev.shape, F32),
                   jax.ShapeDtypeStruct(conv_prev.shape, F32)],
        scratch_shapes=[pltpu.VMEM((nreq, 8 + t, DN_CONV_DIM), F32)],
        compiler_params=_params(("arbitrary",)),
        name="deltanet_sample",
    )(qkv.reshape(b, t, DN_CONV_DIM), z, ab, conv_prev, ssm_prev, conv_w, alog, dtb, ng, lt, sext)


def _sb_tile(z, vis, u01, carry):
    lk = jnp.minimum(-z, 0.0) - jnp.log1p(jnp.exp(-jnp.abs(z)))
    if vis is not None:
        lk = jnp.where(vis, lk, 0.0)
    hi, lo = _split2(lk)
    cum = _dot(hi, u01) + _dot(lo, u01)
    w = jnp.exp(z + cum + carry)
    if vis is not None:
        w = jnp.where(vis, w, 0.0)
    return w.astype(BF16), carry + cum[:, 0:1]


def _upper01(tk):
    j = jnp.arange(tk)[:, None]
    s = jnp.arange(tk)[None, :]
    return (j >= s).astype(BF16)


def _sbp_kernel(bias_ref, q_ref, k_ref, v_ref, u_ref, g_ref, o_ref, *, tq):
    p = pl.program_id(0)
    i = pl.program_id(1)
    lane = lax.broadcasted_iota(jnp.int32, (1, LANES), 1)
    head_lanes = [lane < SB_HEAD_DIM, lane >= SB_HEAD_DIM]
    qs = q_ref[...] * (SB_HEAD_DIM ** -0.5)
    qh = [jnp.where(m, qs, 0.0).astype(BF16) for m in head_lanes]
    bias = [bias_ref[2 * p], bias_ref[2 * p + 1]]
    u01 = u_ref[...]
    row = lax.broadcasted_iota(jnp.int32, (tq, tq), 0)
    col = lax.broadcasted_iota(jnp.int32, (tq, tq), 1)
    vis = col < row

    def tile(j, state, mask):
        start = pl.multiple_of(j * tq, tq)
        kt = k_ref[:, pl.ds(start, tq)]
        vt = v_ref[:, pl.ds(start, tq)]
        out = []
        for hh in range(2):
            carry, acc = state[hh]
            z = _dot(qh[hh], kt) + bias[hh]
            w, carry = _sb_tile(z, mask, u01, carry)
            out.append((carry, acc + _dot(w, vt, NT)))
        return tuple(out)

    zero = (jnp.zeros((tq, 1), F32), jnp.zeros((tq, LANES), F32))
    state = tile(i, (zero, zero), vis)
    state = lax.fori_loop(0, i, lambda it, st: tile(i - 1 - it, st, None), state)
    out = jnp.where(head_lanes[0], state[0][1], state[1][1])
    sq = out * out
    ms = [jnp.sum(jnp.where(m, sq, 0.0), axis=-1, keepdims=True) * (1.0 / SB_HEAD_DIM) for m in head_lanes]
    ms = jnp.where(head_lanes[0], ms[0], ms[1])
    o_ref[...] = out * lax.rsqrt(ms + NORM_EPS) * g_ref[...]


def _sb_prompt(q, kt_bf, vt_bf, bias, g2, tq=256):
    t = q.shape[0]
    npairs = SB_WIDTH // LANES
    u01 = _upper01(tq)
    return pl.pallas_call(
        functools.partial(_sbp_kernel, tq=tq),
        grid=(npairs, t // tq),
        in_specs=[pl.BlockSpec(memory_space=pltpu.SMEM),
                  pl.BlockSpec((tq, LANES), lambda p, i: (i, p)),
                  pl.BlockSpec((LANES, t), lambda p, i: (p, 0)),
                  pl.BlockSpec((LANES, t), lambda p, i: (p, 0)),
                  pl.BlockSpec((tq, tq), lambda p, i: (0, 0)),
                  pl.BlockSpec((1, LANES), lambda p, i: (0, 0))],
        out_specs=pl.BlockSpec((tq, LANES), lambda p, i: (i, p)),
        out_shape=jax.ShapeDtypeStruct((t, SB_WIDTH), F32),
        compiler_params=_params(("arbitrary", "arbitrary")),
        name="sb_prompt",
    )(bias, q, kt_bf, vt_bf, u01, g2)


SB_PAGES_PER_STEP = 8


def _sbs_kernel(pt_ref, q_ref, kn_ref, vn_ref, brow_ref, u_ref, g_ref, *rest, t, page):
    npg = SB_PAGES_PER_STEP
    k_refs = rest[0:npg]
    v_refs = rest[npg:2 * npg]
    o_ref, qbd_scr, acc_scr, carry_scr = rest[2 * npg:]
    del pt_ref
    g = pl.program_id(1)
    rows = SB_HEADS * t
    t_shift = t.bit_length() - 1
    head_shift = SB_HEAD_DIM.bit_length() - 1
    u01 = u_ref[...]
    brow = brow_ref[...]

    def process(kp, vp, vis, keys_on_lanes):
        z = _dot(qbd_scr[...], kp.astype(BF16), NN if keys_on_lanes else NT) + brow
        w, carry = _sb_tile(z, vis, u01, carry_scr[...])
        carry_scr[...] = carry
        acc_scr[...] += _dot(w, vp.astype(BF16), NT if keys_on_lanes else NN)

    @pl.when(g == 0)
    def _():
        row = lax.broadcasted_iota(jnp.int32, (rows, SB_WIDTH), 0)
        col = lax.broadcasted_iota(jnp.int32, (rows, SB_WIDTH), 1)
        q8 = q_ref[0] * (SB_HEAD_DIM ** -0.5)
        qt = jnp.concatenate([q8] * SB_HEADS, axis=0)
        qbd_scr[...] = jnp.where((row >> t_shift) == (col >> head_shift), qt, 0.0).astype(BF16)
        acc_scr[...] = jnp.zeros_like(acc_scr)
        carry_scr[...] = jnp.zeros_like(carry_scr)
        pad = jnp.zeros((page - t, SB_WIDTH), F32)
        krow = lax.broadcasted_iota(jnp.int32, (rows, page), 0)
        kcol = lax.broadcasted_iota(jnp.int32, (rows, page), 1)
        process(jnp.concatenate([kn_ref[0], pad], axis=0),
                jnp.concatenate([vn_ref[0], pad], axis=0), kcol < (krow & (t - 1)), False)

    for i in range(npg):
        process(k_refs[i][0], v_refs[i][0], None, True)

    @pl.when(g == pl.num_programs(1) - 1)
    def _():
        acc = acc_scr[...]
        col = lax.broadcasted_iota(jnp.int32, (t, SB_WIDTH), 1)
        out = jnp.zeros((t, SB_WIDTH), F32)
        for h in range(SB_HEADS):
            out = out + jnp.where((col >> head_shift) == h, acc[h * t:(h + 1) * t, :], 0.0)
        sq = out * out
        ms = jnp.zeros((t, SB_WIDTH), F32)
        for h in range(SB_HEADS):
            in_h = (col >> head_shift) == h
            ms_h = jnp.sum(jnp.where(in_h, sq, 0.0), axis=-1, keepdims=True) * (1.0 / SB_HEAD_DIM)
            ms = jnp.where(in_h, ms_h, ms)
        o_ref[0] = out * lax.rsqrt(ms + NORM_EPS) * g_ref[...]


def _sb_sample(q, k_new, v_new, pool_k, pool_v, page_table, bias, g8):
    b, t, _ = q.shape
    n_pages = page_table.shape[1]
    page = pool_k.shape[2]
    npg = SB_PAGES_PER_STEP
    assert n_pages % npg == 0 and page == LANES and t == 8
    rows = SB_HEADS * t
    brow = jnp.broadcast_to(jnp.repeat(bias, t)[:, None], (rows, LANES))
    u01 = _upper01(page)
    pt_flat = page_table.reshape(-1)

    def page_spec(i):
        return pl.BlockSpec(
            (1, SB_WIDTH, page),
            lambda r, g, pt: (pt[r * n_pages + (n_pages - 1 - (g * npg + i))], 0, 0))

    req_spec = pl.BlockSpec((1, t, SB_WIDTH), lambda r, g, pt: (r, 0, 0))
    grid_spec = pltpu.PrefetchScalarGridSpec(
        num_scalar_prefetch=1,
        grid=(b, n_pages // npg),
        in_specs=[req_spec, req_spec, req_spec,
                  pl.BlockSpec((rows, LANES), lambda r, g, pt: (0, 0)),
                  pl.BlockSpec((page, page), lambda r, g, pt: (0, 0)),
                  pl.BlockSpec((1, SB_WIDTH), lambda r, g, pt: (0, 0))]
                 + [page_spec(i) for i in range(npg)] + [page_spec(i) for i in range(npg)],
        out_specs=req_spec,
        scratch_shapes=[pltpu.VMEM((rows, SB_WIDTH), BF16),
                        pltpu.VMEM((rows, SB_WIDTH), F32),
                        pltpu.VMEM((rows, 1), F32)])
    return pl.pallas_call(
        functools.partial(_sbs_kernel, t=t, page=page),
        grid_spec=grid_spec,
        out_shape=jax.ShapeDtypeStruct((b, t, SB_WIDTH), F32),
        compiler_params=_params(("arbitrary", "arbitrary")),
        name="sb_sample",
    )(pt_flat, q, k_new, v_new, brow, u01, g8, *([pool_k] * npg), *([pool_v] * npg))


def _split_w_in(w_in):
    d = w_in.shape[0]
    o_ab = DN_CONV_DIM + DN_WIDTH
    o_q = o_ab + 2 * DN_HEADS
    o_k = o_q + SB_WIDTH
    o_v = o_k + SB_WIDTH
    pad = jnp.zeros((d, LANES - 2 * DN_HEADS), w_in.dtype)
    w_r = jnp.concatenate([w_in[:, 0:o_ab], w_in[:, o_q:o_k], w_in[:, o_ab:o_q], pad], axis=1)
    return w_r.astype(BF16), w_in[:, o_k:o_v].astype(BF16), w_in[:, o_v:].astype(BF16)


def _heads_last(xt, n):
    return jnp.transpose(xt.reshape(SB_HEADS, SB_HEAD_DIM, n), (2, 0, 1))


def _lane_row(v):
    return jnp.pad(v.astype(F32), (0, LANES - v.shape[0])).reshape(1, LANES)


def kernel(x_prompt, x_sample, c_prompt, c_sample, cache_sb_k, cache_sb_v, page_table, state_dn_ssm, state_dn_conv, w_ada, b_ada, g_ffn1, w1_gate, w1_up, w1_down, g_mix, w_in, dn_conv_w, dn_a_log, dn_dt_bias, dn_norm_g, sb_norm_g, sb_bias, w_out, g_ffn2, w2_gate, w2_up, w2_down, g_final):
    depth = w_ada.shape[0]
    bsz, seq, d = x_prompt.shape
    dec_b, dec_t, _ = x_sample.shape
    assert bsz == 1
    xp, xs = x_prompt, x_sample
    c_all = jnp.concatenate([c_prompt, c_sample], axis=0)
    n_c = c_all.shape[0]
    c_all = jnp.pad(c_all, ((0, (-n_c) % 8), (0, 0)))
    outs = {name: [] for name in ("kp", "vp", "ssmp", "convp", "ks", "vs", "ssms", "convs")}
    for l in range(depth):
        last = l == depth - 1
        mods = _adaln(c_all, w_ada[l].astype(BF16), b_ada[l]).reshape(-1, N_MOD, d)
        mp = [mods[0:bsz, i:i + 1, :] for i in range(N_MOD)]
        ms = [mods[bsz:bsz + dec_b, i:i + 1, :] for i in range(N_MOD)]
        w1 = (w1_gate[l].astype(BF16), w1_up[l].astype(BF16), w1_down[l].astype(BF16))
        w2 = (w2_gate[l].astype(BF16), w2_up[l].astype(BF16), w2_down[l].astype(BF16))
        w_in_r, w_k, w_v = _split_w_in(w_in[l])
        w_out_b = w_out[l].astype(BF16)
        alog = _lane_row(dn_a_log[l])
        dtb = _lane_row(dn_dt_bias[l])
        ng = dn_norm_g[l].reshape(1, DN_HEAD_DIM).astype(F32)
        g_sb = sb_norm_g[l].astype(F32)
        g2 = jnp.tile(g_sb, LANES // SB_HEAD_DIM).reshape(1, LANES)
        g8 = jnp.tile(g_sb, SB_HEADS).reshape(1, SB_WIDTH)
        bias = sb_bias[l].astype(F32)

        xp = _ffn(xp, mp[0], mp[1], mp[2], g_ffn1[l], *w1)
        qkv, z, q, ab, kt, vt, kt_bf, vt_bf = _proj(xp, mp[3], mp[4], g_mix[l], w_in_r, w_k.T, w_v.T, True)
        o_dn, ssm_p, conv_p = _dn_prompt(qkv, z, ab, dn_conv_w[l], alog, dtb, ng)
        o_sb = _sb_prompt(q, kt_bf, vt_bf, bias, g2)
        xp = _outproj(xp, mp[5], o_dn, o_sb, w_out_b)
        xp = _ffn(xp, mp[6], mp[7], mp[8], g_ffn2[l], *w2, gfin=g_final if last else None)
        outs["kp"].append(_heads_last(kt, seq).reshape(bsz, seq, SB_HEADS, SB_HEAD_DIM))
        outs["vp"].append(_heads_last(vt, seq).reshape(bsz, seq, SB_HEADS, SB_HEAD_DIM))
        outs["ssmp"].append(ssm_p.reshape(bsz, DN_HEADS, DN_HEAD_DIM, DN_HEAD_DIM))
        outs["convp"].append(conv_p.reshape(bsz, DN_CONV_W - 1, DN_CONV_DIM))

        xs = _ffn(xs, ms[0], ms[1], ms[2], g_ffn1[l], *w1)
        qkv, z, q, ab, k, v = _proj(xs, ms[3], ms[4], g_mix[l], w_in_r, w_k, w_v, False)
        o_dn, ssm_s, conv_s = _dn_sample(qkv, z, ab, state_dn_conv[l], state_dn_ssm[l],
                                         dn_conv_w[l], alog, dtb, ng, dec_t)
        n_pool, page = cache_sb_k.shape[1], cache_sb_k.shape[2]
        pool_k = jnp.transpose(cache_sb_k[l], (0, 2, 3, 1)).reshape(n_pool, SB_WIDTH, page)
        pool_v = jnp.transpose(cache_sb_v[l], (0, 2, 3, 1)).reshape(n_pool, SB_WIDTH, page)
        o_sb = _sb_sample(q.reshape(dec_b, dec_t, SB_WIDTH), k.reshape(dec_b, dec_t, SB_WIDTH),
                          v.reshape(dec_b, dec_t, SB_WIDTH), pool_k, pool_v, page_table, bias, g8)
        xs = _outproj(xs, ms[5], o_dn, o_sb.reshape(dec_b * dec_t, SB_WIDTH), w_out_b)
        xs = _ffn(xs, ms[6], ms[7], ms[8], g_ffn2[l], *w2, gfin=g_final if last else None)
        outs["ks"].append(k.reshape(dec_b, dec_t, SB_HEADS, SB_HEAD_DIM))
        outs["vs"].append(v.reshape(dec_b, dec_t, SB_HEADS, SB_HEAD_DIM))
        outs["ssms"].append(ssm_s)
        outs["convs"].append(conv_s)
    if depth == 0:
        raise ValueError("depth must be positive")
    return (xp, xs,
            jnp.stack(outs["kp"]), jnp.stack(outs["vp"]), jnp.stack(outs["ssmp"]), jnp.stack(outs["convp"]),
            jnp.stack(outs["ks"]), jnp.stack(outs["vs"]), jnp.stack(outs["ssms"]), jnp.stack(outs["convs"]))
```

```python
import functools

import jax
import jax.numpy as jnp
from jax import lax
from jax.experimental import pallas as pl
from jax.experimental.pallas import tpu as pltpu

F32 = jnp.float32
BF16 = jnp.bfloat16

DN_HEADS = 4
DN_HEAD_DIM = 128
DN_WIDTH = DN_HEADS * DN_HEAD_DIM
DN_CONV_W = 4
DN_CONV_DIM = 3 * DN_WIDTH
DN_CHUNK = 64
SB_HEADS = 8
SB_HEAD_DIM = 64
SB_WIDTH = SB_HEADS * SB_HEAD_DIM
N_MOD = 9
NORM_EPS = 1e-6
L2_EPS = 1e-6
LANES = 128

NN = (((1,), (0,)), ((), ()))
NT = (((1,), (1,)), ((), ()))
TN = (((0,), (0,)), ((), ()))

VMEM_LIMIT = 56 * 1024 * 1024


def _params(sem):
    return pltpu.CompilerParams(dimension_semantics=sem, vmem_limit_bytes=VMEM_LIMIT)


def _dot(a, b, dims=NN):
    return lax.dot_general(a, b, dims, preferred_element_type=F32)


def _split2(x):
    hi = x.astype(BF16)
    lo = (x - hi.astype(F32)).astype(BF16)
    return hi, lo


def _split3(x):
    hi = x.astype(BF16)
    r = x - hi.astype(F32)
    mid = r.astype(BF16)
    lo = (r - mid.astype(F32)).astype(BF16)
    return hi, mid, lo


def _mm3(a, b, dims=NN):
    ah, al = _split2(a)
    bh, bl = _split2(b)
    return _dot(ah, bh, dims) + (_dot(ah, bl, dims) + _dot(al, bh, dims))


def _mm_mask_left(m01, x):
    hi, mid, lo = _split3(x)
    return _dot(m01, hi) + (_dot(m01, mid) + _dot(m01, lo))


def _sigmoid(x):
    return 1.0 / (1.0 + jnp.exp(-x))


def _silu(x):
    return x * _sigmoid(x)


def _softplus(x):
    return jnp.maximum(x, 0.0) + jnp.log1p(jnp.exp(-jnp.abs(x)))


def _modnorm(x, gain, shift, scale):
    y = x * lax.rsqrt(jnp.mean(x * x, axis=-1, keepdims=True) + NORM_EPS)
    return (y * gain) * (1.0 + scale) + shift


def _adaln_kernel(c_ref, w_ref, b_ref, o_ref):
    c = _silu(c_ref[...]).astype(BF16)
    o_ref[...] = _dot(c, w_ref[...]) + b_ref[...]


def _adaln(c, w_bf, b):
    n, d = c.shape
    cols = w_bf.shape[1]
    tn = d
    return pl.pallas_call(
        _adaln_kernel,
        grid=(cols // tn,),
        in_specs=[pl.BlockSpec((n, d), lambda j: (0, 0)),
                  pl.BlockSpec((d, tn), lambda j: (0, j)),
                  pl.BlockSpec((1, tn), lambda j: (0, j))],
        out_specs=pl.BlockSpec((n, tn), lambda j: (0, j)),
        out_shape=jax.ShapeDtypeStruct((n, cols), F32),
        compiler_params=_params(("arbitrary",)),
        name="adaln",
    )(c, w_bf, b.reshape(1, cols))


def _row_blocks(g, r, target):
    if r >= target:
        assert r % target == 0
        return 1, target
    gb = min(g, target // r)
    assert g % gb == 0
    return gb, r


MXU_WIDTH = 256


def _ffn_kernel(x_ref, sh_ref, sc_ref, gt_ref, gain_ref, wg_ref, wu_ref, wd_ref, *rest, final_norm):
    if final_norm:
        gfin_ref, o_ref, acc_scr = rest
    else:
        o_ref, acc_scr = rest
    rows, d = acc_scr.shape
    x = x_ref[...]
    h = _modnorm(x, gain_ref[...], sh_ref[...], sc_ref[...]).reshape(rows, d).astype(BF16)
    dff = wg_ref.shape[1]
    for f in range(dff // MXU_WIDTH):
        cols = slice(f * MXU_WIDTH, (f + 1) * MXU_WIDTH)
        a = _dot(h, wg_ref[:, cols])
        b = _dot(h, wu_ref[:, cols])
        part = _dot((_silu(a) * b).astype(BF16), wd_ref[cols, :])
        if f == 0:
            acc_scr[...] = part
        else:
            acc_scr[...] += part
    y = x + 0.5 * gt_ref[...] * acc_scr[...].reshape(x.shape)
    if final_norm:
        y = y * lax.rsqrt(jnp.mean(y * y, axis=-1, keepdims=True) + NORM_EPS) * gfin_ref[...]
    o_ref[...] = y


def _ffn(x, shift, scale, gate, gain, wg, wu, wd, gfin=None, rows_target=512):
    g, r, d = x.shape
    dff = wg.shape[1]
    assert dff % MXU_WIDTH == 0
    gb, rb = _row_blocks(g, r, rows_target)
    rows = gb * rb
    mod_spec = pl.BlockSpec((gb, 1, d), lambda i, j: (i, 0, 0))
    vec_spec = pl.BlockSpec((1, d), lambda i, j: (0, 0))
    resident = lambda a: pl.BlockSpec(a.shape, lambda i, j: (0, 0), pipeline_mode=pl.Buffered(1))
    in_specs = [pl.BlockSpec((gb, rb, d), lambda i, j: (i, j, 0)),
                mod_spec, mod_spec, mod_spec, vec_spec,
                resident(wg), resident(wu), resident(wd)]
    args = [x, shift, scale, gate, gain.reshape(1, d), wg, wu, wd]
    if gfin is not None:
        in_specs.append(vec_spec)
        args.append(gfin.reshape(1, d))
    return pl.pallas_call(
        functools.partial(_ffn_kernel, final_norm=gfin is not None),
        grid=(g // gb, r // rb),
        in_specs=in_specs,
        out_specs=pl.BlockSpec((gb, rb, d), lambda i, j: (i, j, 0)),
        out_shape=jax.ShapeDtypeStruct(x.shape, F32),
        scratch_shapes=[pltpu.VMEM((rows, d), F32)],
        compiler_params=_params(("arbitrary", "arbitrary")),
        name="ffn",
    )(*args)


PROJ_COLS = (DN_CONV_DIM, DN_WIDTH, SB_WIDTH, LANES)


def _proj_kernel(x_ref, sh_ref, sc_ref, gain_ref, w_ref, wk_ref, wv_ref, *o_refs, transposed):
    rows = o_refs[0].shape[0]
    h = _modnorm(x_ref[...], gain_ref[...], sh_ref[...], sc_ref[...])
    h = h.reshape(rows, h.shape[-1]).astype(BF16)
    off = 0
    for o_ref, cols in zip(o_refs[0:len(PROJ_COLS)], PROJ_COLS):
        o_ref[...] = _dot(h, w_ref[:, off:off + cols])
        off += cols
    if transposed:
        k_ref, v_ref, kb_ref, vb_ref = o_refs[len(PROJ_COLS):]
        kt = _dot(wk_ref[...], h, NT)
        vt = _dot(wv_ref[...], h, NT)
        k_ref[...] = kt
        v_ref[...] = vt
        kb_ref[...] = kt.astype(BF16)
        vb_ref[...] = vt.astype(BF16)
    else:
        k_ref, v_ref = o_refs[len(PROJ_COLS):]
        k_ref[...] = _dot(h, wk_ref[...])
        v_ref[...] = _dot(h, wv_ref[...])


def _proj(x, shift, scale, gain, w_r, w_k, w_v, transposed, rows_target=512):
    g, r, d = x.shape
    gb, rb = _row_blocks(g, r, rows_target)
    rows = gb * rb
    nj = r // rb
    n = g * r
    mod_spec = pl.BlockSpec((gb, 1, d), lambda i, j: (i, 0, 0))
    const = lambda a: pl.BlockSpec(a.shape, lambda i, j: (0, 0))
    out_specs = [pl.BlockSpec((rows, c), lambda i, j: (i * nj + j, 0)) for c in PROJ_COLS]
    out_shape = [jax.ShapeDtypeStruct((n, c), F32) for c in PROJ_COLS]
    if transposed:
        out_specs += [pl.BlockSpec((SB_WIDTH, rows), lambda i, j: (0, i * nj + j))] * 4
        out_shape += [jax.ShapeDtypeStruct((SB_WIDTH, n), dt) for dt in (F32, F32, BF16, BF16)]
    else:
        out_specs += [pl.BlockSpec((rows, SB_WIDTH), lambda i, j: (i * nj + j, 0))] * 2
        out_shape += [jax.ShapeDtypeStruct((n, SB_WIDTH), F32)] * 2
    return pl.pallas_call(
        functools.partial(_proj_kernel, transposed=transposed),
        grid=(g // gb, nj),
        in_specs=[pl.BlockSpec((gb, rb, d), lambda i, j: (i, j, 0)),
                  mod_spec, mod_spec,
                  pl.BlockSpec((1, d), lambda i, j: (0, 0)),
                  const(w_r), const(w_k), const(w_v)],
        out_specs=out_specs,
        out_shape=out_shape,
        compiler_params=_params(("arbitrary", "arbitrary")),
        name="proj_in",
    )(x, shift, scale, gain.reshape(1, d), w_r, w_k, w_v)


def _outproj_kernel(x_ref, gt_ref, odn_ref, osb_ref, w_ref, o_ref):
    x = x_ref[...]
    mixed = (_dot(odn_ref[...].astype(BF16), w_ref[0:DN_WIDTH, :])
             + _dot(osb_ref[...].astype(BF16), w_ref[DN_WIDTH:DN_WIDTH + SB_WIDTH, :]))
    o_ref[...] = x + gt_ref[...] * mixed.reshape(x.shape)


def _outproj(x, gate, o_dn, o_sb, w_out, rows_target=512):
    g, r, d = x.shape
    gb, rb = _row_blocks(g, r, rows_target)
    rows = gb * rb
    nj = r // rb
    return pl.pallas_call(
        _outproj_kernel,
        grid=(g // gb, nj),
        in_specs=[pl.BlockSpec((gb, rb, d), lambda i, j: (i, j, 0)),
                  pl.BlockSpec((gb, 1, d), lambda i, j: (i, 0, 0)),
                  pl.BlockSpec((rows, DN_WIDTH), lambda i, j: (i * nj + j, 0)),
                  pl.BlockSpec((rows, SB_WIDTH), lambda i, j: (i * nj + j, 0)),
                  pl.BlockSpec(w_out.shape, lambda i, j: (0, 0))],
        out_specs=pl.BlockSpec((gb, rb, d), lambda i, j: (i, j, 0)),
        out_shape=jax.ShapeDtypeStruct(x.shape, F32),
        compiler_params=_params(("arbitrary", "arbitrary")),
        name="proj_out",
    )(x, gate, o_dn, o_sb, w_out)


def _dn_masks(group):
    c = DN_CHUNK
    i = jnp.arange(c)[:, None]
    m = jnp.arange(c)[None, :]
    same = (i // group) == (m // group)
    lower = jnp.logical_and(m <= i, same)
    lt = jnp.concatenate([lower, same], axis=0).astype(BF16)
    sext = jnp.concatenate([(i > m), jnp.ones((c, c), bool)], axis=1).astype(F32)
    return lt, sext


def _dn_gates(ab, alog, dtb):
    g = -jnp.exp(alog) * _softplus(ab + dtb)
    beta = _sigmoid(ab)
    return g, beta


def _dn_local(items, lt, sext, group):
    c = DN_CHUNK
    row = lax.broadcasted_iota(jnp.int32, (c, c), 0)
    col = lax.broadcasted_iota(jnp.int32, (c, c), 1)
    lower = col <= row
    strict = col < row
    if group < c:
        shift = group.bit_length() - 1
        same = (row >> shift) == (col >> shift)
        lower = jnp.logical_and(lower, same)
        strict = jnp.logical_and(strict, same)
    eye = (row == col).astype(F32)
    n_sq = max(1, (min(group, c) - 1).bit_length() - 1)

    qs, ks, kbs, dms, gams, tots = [], [], [], [], [], []
    for q_raw, k_raw, _, g_col, beta_col in items:
        qs.append(q_raw * lax.rsqrt(jnp.sum(q_raw * q_raw, axis=-1, keepdims=True) + L2_EPS)
                  * (DN_HEAD_DIM ** -0.5))
        k = k_raw * lax.rsqrt(jnp.sum(k_raw * k_raw, axis=-1, keepdims=True) + L2_EPS)
        ks.append(k)
        kbs.append(k * beta_col)
    for _, _, _, g_col, _ in items:
        r = _mm_mask_left(lt, g_col * sext)
        diff = r[0:c, 0:c]
        gams.append(r[0:c, c:c + 1])
        tots.append(r[c:2 * c, c:c + 1])
        dms.append(jnp.where(lower, jnp.exp(jnp.where(lower, diff, 0.0)), 0.0))
    a_s = [jnp.where(strict, _mm3(kb, k, NT) * dm, 0.0) for kb, k, dm in zip(kbs, ks, dms)]
    attns = [_mm3(q, k, NT) * dm for q, k, dm in zip(qs, ks, dms)]
    ps = [eye - a for a in a_s]
    xs = a_s
    for _ in range(n_sq):
        xs = [_mm3(x, x) for x in xs]
        ps = [p + _mm3(p, x) for p, x in zip(ps, xs)]
    egs = [jnp.exp(gam) for gam in gams]
    uws = [_mm3(p, jnp.concatenate([it[2] * it[4], kb * eg], axis=1))
           for p, it, kb, eg in zip(ps, items, kbs, egs)]
    out = []
    for q, k, uw, attn, eg, gam, tot in zip(qs, ks, uws, attns, egs, gams, tots):
        out.append((uw[:, 0:DN_HEAD_DIM], uw[:, DN_HEAD_DIM:2 * DN_HEAD_DIM], attn,
                    q * eg, k * jnp.exp(tot - gam), tot))
    return out


def _dn_items(qkv, g_all, beta_all, r0):
    c = DN_CHUNK
    dh = DN_HEAD_DIM
    items = []
    for h in range(DN_HEADS):
        items.append((qkv[r0:r0 + c, h * dh:(h + 1) * dh],
                      qkv[r0:r0 + c, DN_WIDTH + h * dh:DN_WIDTH + (h + 1) * dh],
                      qkv[r0:r0 + c, 2 * DN_WIDTH + h * dh:2 * DN_WIDTH + (h + 1) * dh],
                      g_all[r0:r0 + c, h:h + 1],
                      beta_all[r0:r0 + c, DN_HEADS + h:DN_HEADS + h + 1]))
    return items


def _dn_out(o, z, ng):
    y = o * lax.rsqrt(jnp.mean(o * o, axis=-1, keepdims=True) + NORM_EPS) * ng
    return y * _silu(z)


def _dn_prompt_kernel(qkv_ref, z_ref, ab_ref, cw_ref, alog_ref, dtb_ref, ng_ref, lt_ref, sext_ref,
                      o_ref, s_out_ref, conv_out_ref, xp_scr, s_scr):
    c = DN_CHUNK
    dh = DN_HEAD_DIM
    tb = qkv_ref.shape[0]
    step = pl.program_id(0)

    @pl.when(step == 0)
    def _():
        xp_scr[0:8, :] = jnp.zeros((8, DN_CONV_DIM), F32)
        s_scr[...] = jnp.zeros_like(s_scr)

    x = qkv_ref[...]
    xp_scr[8:8 + tb, :] = x
    cw = cw_ref[...]
    y = xp_scr[5:5 + tb, :] * cw[0:1, :]
    y = y + xp_scr[6:6 + tb, :] * cw[1:2, :]
    y = y + xp_scr[7:7 + tb, :] * cw[2:3, :]
    y = y + x * cw[3:4, :]
    conv_out_ref[...] = xp_scr[5 + tb:8 + tb, :]
    xp_scr[0:8, :] = xp_scr[tb:tb + 8, :]
    qkv = _silu(y)

    g_all, beta_all = _dn_gates(ab_ref[...], alog_ref[...], dtb_ref[...])
    ng = ng_ref[...]
    items = []
    for n in range(tb // c):
        items += _dn_items(qkv, g_all, beta_all, n * c)
    local = _dn_local(items, lt_ref[...], sext_ref[...], c)
    s = [s_scr[h] for h in range(DN_HEADS)]
    for n in range(tb // c):
        loc = local[n * DN_HEADS:(n + 1) * DN_HEADS]
        v_new = [u - _mm3(w, s[h]) for h, (u, w, _, _, _, _) in enumerate(loc)]
        o = [_mm3(q_dec, s[h]) + _mm3(attn, v_new[h]) for h, (_, _, attn, q_dec, _, _) in enumerate(loc)]
        s = [s[h] * jnp.exp(tot[0:1, :]) + _mm3(k_dec, v_new[h], TN)
             for h, (_, _, _, _, k_dec, tot) in enumerate(loc)]
        for h in range(DN_HEADS):
            o_ref[n * c:(n + 1) * c, h * dh:(h + 1) * dh] = _dn_out(
                o[h], z_ref[n * c:(n + 1) * c, h * dh:(h + 1) * dh], ng)
    for h in range(DN_HEADS):
        s_scr[h] = s[h]
        s_out_ref[h] = s[h]


def _dn_prompt(qkv, z, ab, conv_w, alog, dtb, ng, chunks_per_step=4):
    t = qkv.shape[0]
    c = DN_CHUNK * min(chunks_per_step, t // DN_CHUNK)
    assert t % c == 0
    lt, sext = _dn_masks(DN_CHUNK)
    const = lambda shape: pl.BlockSpec(shape, lambda i: tuple(0 for _ in shape))
    return pl.pallas_call(
        _dn_prompt_kernel,
        grid=(t // c,),
        in_specs=[pl.BlockSpec((c, DN_CONV_DIM), lambda i: (i, 0)),
                  pl.BlockSpec((c, DN_WIDTH), lambda i: (i, 0)),
                  pl.BlockSpec((c, LANES), lambda i: (i, 0)),
                  const((DN_CONV_W, DN_CONV_DIM)), const((1, LANES)), const((1, LANES)),
                  const((1, DN_HEAD_DIM)), const(lt.shape), const(sext.shape)],
        out_specs=[pl.BlockSpec((c, DN_WIDTH), lambda i: (i, 0)),
                   const((DN_HEADS, DN_HEAD_DIM, DN_HEAD_DIM)),
                   const((DN_CONV_W - 1, DN_CONV_DIM))],
        out_shape=[jax.ShapeDtypeStruct((t, DN_WIDTH), F32),
                   jax.ShapeDtypeStruct((DN_HEADS, DN_HEAD_DIM, DN_HEAD_DIM), F32),
                   jax.ShapeDtypeStruct((DN_CONV_W - 1, DN_CONV_DIM), F32)],
        scratch_shapes=[pltpu.VMEM((c + 8, DN_CONV_DIM), F32),
                        pltpu.VMEM((DN_HEADS, DN_HEAD_DIM, DN_HEAD_DIM), F32)],
        compiler_params=_params(("arbitrary",)),
        name="deltanet_prompt",
    )(qkv, z, ab, conv_w, alog, dtb, ng, lt, sext)


def _dn_sample_kernel(qkv_ref, z_ref, ab_ref, cprev_ref, s_ref, cw_ref, alog_ref, dtb_ref, ng_ref,
                      lt_ref, sext_ref, o_ref, s_out_ref, conv_out_ref, xp_scr, *, t):
    c = DN_CHUNK
    dh = DN_HEAD_DIM
    nreq = c // t
    x3 = qkv_ref[...]
    xp_scr[:, 8 - (DN_CONV_W - 1):8, :] = cprev_ref[...]
    xp_scr[:, 8:8 + t, :] = x3
    cw = cw_ref[...]
    y = xp_scr[:, 5:5 + t, :] * cw[0:1, :]
    y = y + xp_scr[:, 6:6 + t, :] * cw[1:2, :]
    y = y + xp_scr[:, 7:7 + t, :] * cw[2:3, :]
    y = y + x3 * cw[3:4, :]
    conv_out_ref[...] = xp_scr[:, 5 + t:8 + t, :]
    qkv = _silu(y).reshape(c, DN_CONV_DIM)

    g_all, beta_all = _dn_gates(ab_ref[...], alog_ref[...], dtb_ref[...])
    ng = ng_ref[...]
    rowid = lax.broadcasted_iota(jnp.int32, (c, 1), 0)
    local = _dn_local(_dn_items(qkv, g_all, beta_all, 0), lt_ref[...], sext_ref[...], t)
    for h in range(DN_HEADS):
        u, w, attn, q_dec, k_dec, tot = local[h]
        ws, qs = [], []
        for r in range(nreq):
            s = s_ref[r, h]
            ws.append(_mm3(w, s)[r * t:(r + 1) * t, :])
            qs.append(_mm3(q_dec, s)[r * t:(r + 1) * t, :])
        v_new = u - jnp.concatenate(ws, axis=0)
        o = jnp.concatenate(qs, axis=0) + _mm3(attn, v_new)
        for r in range(nreq):
            in_req = jnp.logical_and(rowid >= r * t, rowid < (r + 1) * t)
            kd_r = jnp.where(in_req, k_dec, 0.0)
            s_out_ref[r, h] = (s_ref[r, h] * jnp.exp(tot[r * t:r * t + 1, :])
                               + _mm3(kd_r, v_new, TN))
        o_ref[:, h * dh:(h + 1) * dh] = _dn_out(o, z_ref[:, h * dh:(h + 1) * dh], ng)


def _dn_sample(qkv, z, ab, conv_prev, ssm_prev, conv_w, alog, dtb, ng, t):
    n = qkv.shape[0]
    b = n // t
    c = DN_CHUNK
    assert t == 8 and c % t == 0 and b % (c // t) == 0
    nreq = c // t
    lt, sext = _dn_masks(t)
    const = lambda shape: pl.BlockSpec(shape, lambda i: tuple(0 for _ in shape))
    return pl.pallas_call(
        functools.partial(_dn_sample_kernel, t=t),
        grid=(b // nreq,),
        in_specs=[pl.BlockSpec((nreq, t, DN_CONV_DIM), lambda i: (i, 0, 0)),
                  pl.BlockSpec((c, DN_WIDTH), lambda i: (i, 0)),
                  pl.BlockSpec((c, LANES), lambda i: (i, 0)),
                  pl.BlockSpec((nreq, DN_CONV_W - 1, DN_CONV_DIM), lambda i: (i, 0, 0)),
                  pl.BlockSpec((nreq, DN_HEADS, DN_HEAD_DIM, DN_HEAD_DIM), lambda i: (i, 0, 0, 0)),
                  const((DN_CONV_W, DN_CONV_DIM)), const((1, LANES)), const((1, LANES)),
                  const((1, DN_HEAD_DIM)), const(lt.shape), const(sext.shape)],
        out_specs=[pl.BlockSpec((c, DN_WIDTH), lambda i: (i, 0)),
                   pl.BlockSpec((nreq, DN_HEADS, DN_HEAD_DIM, DN_HEAD_DIM), lambda i: (i, 0, 0, 0)),
                   pl.BlockSpec((nreq, DN_CONV_W - 1, DN_CONV_DIM), lambda i: (i, 0, 0))],
        out_shape=[jax.ShapeDtypeStruct((n, DN_WIDTH), F32),
                   jax.ShapeDtypeStruct(ssm_prev.shape, F32),
                   jax.ShapeDtypeStruct(conv_prev.shape, F32)],
        scratch_shapes=[pltpu.VMEM((nreq, 8 + t, DN_CONV_DIM), F32)],
        compiler_params=_params(("arbitrary",)),
        name="deltanet_sample",
    )(qkv.reshape(b, t, DN_CONV_DIM), z, ab, conv_prev, ssm_prev, conv_w, alog, dtb, ng, lt, sext)


LOG2E = 1.4426950408889634


def _neg_abs(x):
    sign = jnp.int32(-2 ** 31)
    return lax.bitcast_convert_type(lax.bitcast_convert_type(x, jnp.int32) | sign, F32)


def _sb_softplus(z2, vis):
    sp = jnp.maximum(z2, 0.0) + jnp.log2(1.0 + jnp.exp2(_neg_abs(z2)))
    if vis is not None:
        sp = jnp.where(vis, sp, 0.0)
    return sp.astype(BF16)


def _sb_weights(z2, cum, carry, vis):
    w = jnp.exp2(z2 + cum + carry)
    if vis is not None:
        w = jnp.where(vis, w, 0.0)
    return w.astype(BF16)


def _neg_upper(tk):
    j = jnp.arange(tk)[:, None]
    s = jnp.arange(tk)[None, :]
    return -(j >= s).astype(BF16)


def _sbp_kernel(bias_ref, q_ref, k_ref, v_ref, u_ref, g_ref, o_ref, acc_scr, carry_scr,
                z0_scr, z1_scr, w0_scr, w1_scr, *, tq, tk):
    z_scr = (z0_scr, z1_scr)
    w_scr = (w0_scr, w1_scr)
    p = pl.program_id(0)
    i = pl.program_id(1)
    nsub = tq // tk
    chains = [(hh, r) for hh in range(2) for r in range(nsub)]
    nch = len(chains)
    n_tiles = nsub * (i + 1)
    lane = lax.broadcasted_iota(jnp.int32, (1, LANES), 1)
    head_lanes = [lane < SB_HEAD_DIM, lane >= SB_HEAD_DIM]
    qs = q_ref[...] * (SB_HEAD_DIM ** -0.5 * LOG2E)
    qh = {(hh, r): jnp.where(head_lanes[hh], qs[r * tk:(r + 1) * tk, :], 0.0).astype(BF16)
          for hh, r in chains}
    bias = [bias_ref[2 * p] * LOG2E, bias_ref[2 * p + 1] * LOG2E]
    uneg = u_ref[...]
    col_minus_row = (lax.broadcasted_iota(jnp.int32, (tk, tk), 1)
                     - lax.broadcasted_iota(jnp.int32, (tk, tk), 0))
    acc_scr[...] = jnp.zeros_like(acc_scr)
    carry_scr[...] = jnp.zeros_like(carry_scr)

    def key_tile(n):
        j = jnp.maximum(n_tiles - 1 - n, 0)
        return j, pl.multiple_of(j * tk, tk)

    def logits(n, slot, ci):
        _, start = key_tile(n)
        kt = k_ref[:, pl.ds(start, tk)]
        z_scr[slot][ci] = _dot(qh[chains[ci]], kt) + bias[chains[ci][0]]

    def values(n, slot, ci):
        _, start = key_tile(n)
        vt = v_ref[:, pl.ds(start, tk)]
        acc_scr[ci] += _dot(w_scr[slot][ci], vt, NT)

    def visit(n, slot, masked, with_values):
        j, _ = key_tile(n)
        if masked:
            vis = [col_minus_row < (i * tq + r * tk - j * tk) for _, r in chains]
        else:
            vis = [None] * nch
        zs, sps, cums = [], [], []
        for ci in range(nch):
            zs.append(z_scr[slot][ci])
            sps.append(_sb_softplus(zs[ci], vis[ci]))
            logits(n + 1, 1 - slot, ci)
        for ci in range(nch):
            cums.append(_dot(sps[ci], uneg))
            if with_values:
                values(n - 1, 1 - slot, ci)
        for ci in range(nch):
            w_scr[slot][ci] = _sb_weights(zs[ci], cums[ci], carry_scr[ci], vis[ci])
            carry_scr[ci] += cums[ci][:, 0:1]

    for ci in range(nch):
        logits(0, 0, ci)
    for n in range(nsub):
        visit(n, n % 2, True, n > 0)

    def body(m, carry):
        n = nsub + 2 * m
        for d in range(2):
            visit(n + d, (nsub + d) % 2, False, True)
        return carry

    lax.fori_loop(0, (n_tiles - nsub) // 2, body, 0)
    for ci in range(nch):
        values(n_tiles - 1, (nsub - 1) % 2, ci)
    for r in range(nsub):
        out = jnp.where(head_lanes[0], acc_scr[chains.index((0, r))], acc_scr[chains.index((1, r))])
        sq = out * out
        ms = [jnp.sum(jnp.where(m, sq, 0.0), axis=-1, keepdims=True) * (1.0 / SB_HEAD_DIM)
              for m in head_lanes]
        ms = jnp.where(head_lanes[0], ms[0], ms[1])
        o_ref[r * tk:(r + 1) * tk, :] = out * lax.rsqrt(ms + NORM_EPS) * g_ref[...]


def _sb_prompt(q, kt_bf, vt_bf, bias, g2, tq=512, tk=256):
    t = q.shape[0]
    tq = min(tq, t)
    assert t % tq == 0 and tq % (2 * tk) == 0
    npairs = SB_WIDTH // LANES
    nchains = 2 * (tq // tk)
    return pl.pallas_call(
        functools.partial(_sbp_kernel, tq=tq, tk=tk),
        grid=(npairs, t // tq),
        in_specs=[pl.BlockSpec(memory_space=pltpu.SMEM),
                  pl.BlockSpec((tq, LANES), lambda p, i: (i, p)),
                  pl.BlockSpec((LANES, t), lambda p, i: (p, 0)),
                  pl.BlockSpec((LANES, t), lambda p, i: (p, 0)),
                  pl.BlockSpec((tk, tk), lambda p, i: (0, 0)),
                  pl.BlockSpec((1, LANES), lambda p, i: (0, 0))],
        out_specs=pl.BlockSpec((tq, LANES), lambda p, i: (i, p)),
        out_shape=jax.ShapeDtypeStruct((t, SB_WIDTH), F32),
        scratch_shapes=[pltpu.VMEM((nchains, tk, LANES), F32),
                        pltpu.VMEM((nchains, tk, 1), F32),
                        pltpu.VMEM((nchains, tk, tk), F32), pltpu.VMEM((nchains, tk, tk), F32),
                        pltpu.VMEM((nchains, tk, tk), BF16), pltpu.VMEM((nchains, tk, tk), BF16)],
        compiler_params=_params(("arbitrary", "arbitrary")),
        name="sb_prompt",
    )(bias, q, kt_bf, vt_bf, _neg_upper(tk), g2)


SB_PAGES_PER_STEP = 8


def _sbs_kernel(pt_ref, q_ref, kn_ref, vn_ref, brow_ref, u_ref, g_ref, *rest, t, page):
    npg = SB_PAGES_PER_STEP
    k_refs = rest[0:npg]
    v_refs = rest[npg:2 * npg]
    o_ref, qbd_scr, acc_scr, carry_scr = rest[2 * npg:]
    del pt_ref
    g = pl.program_id(1)
    rows = SB_HEADS * t
    t_shift = t.bit_length() - 1
    head_shift = SB_HEAD_DIM.bit_length() - 1
    uneg = u_ref[...]
    brow = brow_ref[...] * LOG2E

    @pl.when(g == 0)
    def _():
        row = lax.broadcasted_iota(jnp.int32, (rows, SB_WIDTH), 0)
        col = lax.broadcasted_iota(jnp.int32, (rows, SB_WIDTH), 1)
        q8 = q_ref[0] * (SB_HEAD_DIM ** -0.5 * LOG2E)
        qt = jnp.concatenate([q8] * SB_HEADS, axis=0)
        qbd_scr[...] = jnp.where((row >> t_shift) == (col >> head_shift), qt, 0.0).astype(BF16)
        pad = jnp.zeros((page - t, SB_WIDTH), F32)
        kn = jnp.concatenate([kn_ref[0], pad], axis=0).astype(BF16)
        vn = jnp.concatenate([vn_ref[0], pad], axis=0).astype(BF16)
        krow = lax.broadcasted_iota(jnp.int32, (rows, page), 0)
        kcol = lax.broadcasted_iota(jnp.int32, (rows, page), 1)
        vis = kcol < (krow & (t - 1))
        z = _dot(qbd_scr[...], kn, NT) + brow
        cum = _dot(_sb_softplus(z, vis), uneg)
        acc_scr[...] = _dot(_sb_weights(z, cum, 0.0, vis), vn)
        carry_scr[...] = cum[:, 0:1]

    qbd = qbd_scr[...]
    zs = [_dot(qbd, k_refs[i][0].astype(BF16)) + brow for i in range(npg)]
    cums = [_dot(_sb_softplus(z, None), uneg) for z in zs]
    carry = carry_scr[...]
    acc = acc_scr[...]
    for i in range(npg):
        w = _sb_weights(zs[i], cums[i], carry, None)
        acc = acc + _dot(w, v_refs[i][0].astype(BF16), NT)
        carry = carry + cums[i][:, 0:1]
    acc_scr[...] = acc
    carry_scr[...] = carry

    @pl.when(g == pl.num_programs(1) - 1)
    def _():
        acc = acc_scr[...]
        col = lax.broadcasted_iota(jnp.int32, (t, SB_WIDTH), 1)
        out = jnp.zeros((t, SB_WIDTH), F32)
        for h in range(SB_HEADS):
            out = out + jnp.where((col >> head_shift) == h, acc[h * t:(h + 1) * t, :], 0.0)
        sq = out * out
        ms = jnp.zeros((t, SB_WIDTH), F32)
        for h in range(SB_HEADS):
            in_h = (col >> head_shift) == h
            ms_h = jnp.sum(jnp.where(in_h, sq, 0.0), axis=-1, keepdims=True) * (1.0 / SB_HEAD_DIM)
            ms = jnp.where(in_h, ms_h, ms)
        o_ref[0] = out * lax.rsqrt(ms + NORM_EPS) * g_ref[...]


def _sb_sample(q, k_new, v_new, pool_k, pool_v, page_table, bias, g8):
    b, t, _ = q.shape
    n_pages = page_table.shape[1]
    page = pool_k.shape[2]
    npg = SB_PAGES_PER_STEP
    assert n_pages % npg == 0 and page == LANES and t == 8
    rows = SB_HEADS * t
    brow = jnp.broadcast_to(jnp.repeat(bias, t)[:, None], (rows, LANES))
    u01 = _neg_upper(page)
    pt_flat = page_table.reshape(-1)

    def page_spec(i):
        return pl.BlockSpec(
            (1, SB_WIDTH, page),
            lambda r, g, pt: (pt[r * n_pages + (n_pages - 1 - (g * npg + i))], 0, 0))

    req_spec = pl.BlockSpec((1, t, SB_WIDTH), lambda r, g, pt: (r, 0, 0))
    grid_spec = pltpu.PrefetchScalarGridSpec(
        num_scalar_prefetch=1,
        grid=(b, n_pages // npg),
        in_specs=[req_spec, req_spec, req_spec,
                  pl.BlockSpec((rows, LANES), lambda r, g, pt: (0, 0)),
                  pl.BlockSpec((page, page), lambda r, g, pt: (0, 0)),
                  pl.BlockSpec((1, SB_WIDTH), lambda r, g, pt: (0, 0))]
                 + [page_spec(i) for i in range(npg)] + [page_spec(i) for i in range(npg)],
        out_specs=req_spec,
        scratch_shapes=[pltpu.VMEM((rows, SB_WIDTH), BF16),
                        pltpu.VMEM((rows, SB_WIDTH), F32),
                        pltpu.VMEM((rows, 1), F32)])
    return pl.pallas_call(
        functools.partial(_sbs_kernel, t=t, page=page),
        grid_spec=grid_spec,
        out_shape=jax.ShapeDtypeStruct((b, t, SB_WIDTH), F32),
        compiler_params=_params(("arbitrary", "arbitrary")),
        name="sb_sample",
    )(pt_flat, q, k_new, v_new, brow, u01, g8, *([pool_k] * npg), *([pool_v] * npg))


def _split_w_in(w_in):
    d = w_in.shape[0]
    o_ab = DN_CONV_DIM + DN_WIDTH
    o_q = o_ab + 2 * DN_HEADS
    o_k = o_q + SB_WIDTH
    o_v = o_k + SB_WIDTH
    pad = jnp.zeros((d, LANES - 2 * DN_HEADS), w_in.dtype)
    w_r = jnp.concatenate([w_in[:, 0:o_ab], w_in[:, o_q:o_k], w_in[:, o_ab:o_q], pad], axis=1)
    return w_r.astype(BF16), w_in[:, o_k:o_v].astype(BF16), w_in[:, o_v:].astype(BF16)


def _heads_last(xt, n):
    return jnp.transpose(xt.reshape(SB_HEADS, SB_HEAD_DIM, n), (2, 0, 1))


def _lane_row(v):
    return jnp.pad(v.astype(F32), (0, LANES - v.shape[0])).reshape(1, LANES)


def kernel(x_prompt, x_sample, c_prompt, c_sample, cache_sb_k, cache_sb_v, page_table, state_dn_ssm, state_dn_conv, w_ada, b_ada, g_ffn1, w1_gate, w1_up, w1_down, g_mix, w_in, dn_conv_w, dn_a_log, dn_dt_bias, dn_norm_g, sb_norm_g, sb_bias, w_out, g_ffn2, w2_gate, w2_up, w2_down, g_final):
    depth = w_ada.shape[0]
    bsz, seq, d = x_prompt.shape
    dec_b, dec_t, _ = x_sample.shape
    assert bsz == 1
    xp, xs = x_prompt, x_sample
    c_all = jnp.concatenate([c_prompt, c_sample], axis=0)
    n_c = c_all.shape[0]
    c_all = jnp.pad(c_all, ((0, (-n_c) % 8), (0, 0)))
    outs = {name: [] for name in ("kp", "vp", "ssmp", "convp", "ks", "vs", "ssms", "convs")}
    for l in range(depth):
        last = l == depth - 1
        mods = _adaln(c_all, w_ada[l].astype(BF16), b_ada[l]).reshape(-1, N_MOD, d)
        mp = [mods[0:bsz, i:i + 1, :] for i in range(N_MOD)]
        ms = [mods[bsz:bsz + dec_b, i:i + 1, :] for i in range(N_MOD)]
        w1 = (w1_gate[l].astype(BF16), w1_up[l].astype(BF16), w1_down[l].astype(BF16))
        w2 = (w2_gate[l].astype(BF16), w2_up[l].astype(BF16), w2_down[l].astype(BF16))
        w_in_r, w_k, w_v = _split_w_in(w_in[l])
        w_out_b = w_out[l].astype(BF16)
        alog = _lane_row(dn_a_log[l])
        dtb = _lane_row(dn_dt_bias[l])
        ng = dn_norm_g[l].reshape(1, DN_HEAD_DIM).astype(F32)
        g_sb = sb_norm_g[l].astype(F32)
        g2 = jnp.tile(g_sb, LANES // SB_HEAD_DIM).reshape(1, LANES)
        g8 = jnp.tile(g_sb, SB_HEADS).reshape(1, SB_WIDTH)
        bias = sb_bias[l].astype(F32)

        xp = _ffn(xp, mp[0], mp[1], mp[2], g_ffn1[l], *w1)
        qkv, z, q, ab, kt, vt, kt_bf, vt_bf = _proj(xp, mp[3], mp[4], g_mix[l], w_in_r, w_k.T, w_v.T, True)
        o_dn, ssm_p, conv_p = _dn_prompt(qkv, z, ab, dn_conv_w[l], alog, dtb, ng)
        o_sb = _sb_prompt(q, kt_bf, vt_bf, bias, g2)
        xp = _outproj(xp, mp[5], o_dn, o_sb, w_out_b)
        xp = _ffn(xp, mp[6], mp[7], mp[8], g_ffn2[l], *w2, gfin=g_final if last else None)
        outs["kp"].append(_heads_last(kt, seq).reshape(bsz, seq, SB_HEADS, SB_HEAD_DIM))
        outs["vp"].append(_heads_last(vt, seq).reshape(bsz, seq, SB_HEADS, SB_HEAD_DIM))
        outs["ssmp"].append(ssm_p.reshape(bsz, DN_HEADS, DN_HEAD_DIM, DN_HEAD_DIM))
        outs["convp"].append(conv_p.reshape(bsz, DN_CONV_W - 1, DN_CONV_DIM))

        xs = _ffn(xs, ms[0], ms[1], ms[2], g_ffn1[l], *w1)
        qkv, z, q, ab, k, v = _proj(xs, ms[3], ms[4], g_mix[l], w_in_r, w_k, w_v, False)
        o_dn, ssm_s, conv_s = _dn_sample(qkv, z, ab, state_dn_conv[l], state_dn_ssm[l],
                                         dn_conv_w[l], alog, dtb, ng, dec_t)
        n_pool, page = cache_sb_k.shape[1], cache_sb_k.shape[2]
        pool_k = jnp.transpose(cache_sb_k[l], (0, 2, 3, 1)).reshape(n_pool, SB_WIDTH, page)
        pool_v = jnp.transpose(cache_sb_v[l], (0, 2, 3, 1)).reshape(n_pool, SB_WIDTH, page)
        o_sb = _sb_sample(q.reshape(dec_b, dec_t, SB_WIDTH), k.reshape(dec_b, dec_t, SB_WIDTH),
                          v.reshape(dec_b, dec_t, SB_WIDTH), pool_k, pool_v, page_table, bias, g8)
        xs = _outproj(xs, ms[5], o_dn, o_sb.reshape(dec_b * dec_t, SB_WIDTH), w_out_b)
        xs = _ffn(xs, ms[6], ms[7], ms[8], g_ffn2[l], *w2, gfin=g_final if last else None)
        outs["ks"].append(k.reshape(dec_b, dec_t, SB_HEADS, SB_HEAD_DIM))
        outs["vs"].append(v.reshape(dec_b, dec_t, SB_HEADS, SB_HEAD_DIM))
        outs["ssms"].append(ssm_s)
        outs["convs"].append(conv_s)
    if depth == 0:
        raise ValueError("depth must be positive")
    return (xp, xs,
            jnp.stack(outs["kp"]), jnp.stack(outs["vp"]), jnp.stack(outs["ssmp"]), jnp.stack(outs["convp"]),
            jnp.stack(outs["ks"]), jnp.stack(outs["vs"]), jnp.stack(outs["ssms"]), jnp.stack(outs["convs"]))
```

```python
import functools

import jax
import jax.numpy as jnp
from jax import lax
from jax.experimental import pallas as pl
from jax.experimental.pallas import tpu as pltpu

F32 = jnp.float32
BF16 = jnp.bfloat16

DN_HEADS = 4
DN_HEAD_DIM = 128
DN_WIDTH = DN_HEADS * DN_HEAD_DIM
DN_CONV_W = 4
DN_CONV_DIM = 3 * DN_WIDTH
DN_CHUNK = 64
SB_HEADS = 8
SB_HEAD_DIM = 64
SB_WIDTH = SB_HEADS * SB_HEAD_DIM
N_MOD = 9
NORM_EPS = 1e-6
L2_EPS = 1e-6
LANES = 128

NN = (((1,), (0,)), ((), ()))
NT = (((1,), (1,)), ((), ()))
TN = (((0,), (0,)), ((), ()))

VMEM_LIMIT = 56 * 1024 * 1024


def _params(sem):
    return pltpu.CompilerParams(dimension_semantics=sem, vmem_limit_bytes=VMEM_LIMIT)


def _dot(a, b, dims=NN):
    return lax.dot_general(a, b, dims, preferred_element_type=F32)


def _split2(x):
    hi = x.astype(BF16)
    lo = (x - hi.astype(F32)).astype(BF16)
    return hi, lo


def _split3(x):
    hi = x.astype(BF16)
    r = x - hi.astype(F32)
    mid = r.astype(BF16)
    lo = (r - mid.astype(F32)).astype(BF16)
    return hi, mid, lo


def _mm3(a, b, dims=NN):
    ah, al = _split2(a)
    bh, bl = _split2(b)
    return _dot(ah, bh, dims) + (_dot(ah, bl, dims) + _dot(al, bh, dims))


def _mm_mask_left(m01, x):
    hi, mid, lo = _split3(x)
    return _dot(m01, hi) + (_dot(m01, mid) + _dot(m01, lo))


def _sigmoid(x):
    return 1.0 / (1.0 + jnp.exp(-x))


def _silu(x):
    return x * _sigmoid(x)


def _softplus(x):
    return jnp.maximum(x, 0.0) + jnp.log1p(jnp.exp(-jnp.abs(x)))


def _modnorm(x, gain, shift, scale):
    y = x * lax.rsqrt(jnp.mean(x * x, axis=-1, keepdims=True) + NORM_EPS)
    return (y * gain) * (1.0 + scale) + shift


def _adaln_kernel(c_ref, w_ref, b_ref, o_ref):
    c = _silu(c_ref[...]).astype(BF16)
    o_ref[...] = _dot(c, w_ref[...]) + b_ref[...]


def _adaln(c, w_bf, b):
    n, d = c.shape
    cols = w_bf.shape[1]
    tn = d
    return pl.pallas_call(
        _adaln_kernel,
        grid=(cols // tn,),
        in_specs=[pl.BlockSpec((n, d), lambda j: (0, 0)),
                  pl.BlockSpec((d, tn), lambda j: (0, j)),
                  pl.BlockSpec((1, tn), lambda j: (0, j))],
        out_specs=pl.BlockSpec((n, tn), lambda j: (0, j)),
        out_shape=jax.ShapeDtypeStruct((n, cols), F32),
        compiler_params=_params(("arbitrary",)),
        name="adaln",
    )(c, w_bf, b.reshape(1, cols))


def _row_blocks(g, r, target):
    if r >= target:
        assert r % target == 0
        return 1, target
    gb = min(g, target // r)
    assert g % gb == 0
    return gb, r


MXU_WIDTH = 256


def _ffn_kernel(x_ref, sh_ref, sc_ref, gt_ref, gain_ref, wg_ref, wu_ref, wd_ref, *rest, final_norm):
    if final_norm:
        gfin_ref, o_ref, acc_scr = rest
    else:
        o_ref, acc_scr = rest
    rows, d = acc_scr.shape
    x = x_ref[...]
    h = _modnorm(x, gain_ref[...], sh_ref[...], sc_ref[...]).reshape(rows, d).astype(BF16)
    dff = wg_ref.shape[1]
    for f in range(dff // MXU_WIDTH):
        cols = slice(f * MXU_WIDTH, (f + 1) * MXU_WIDTH)
        a = _dot(h, wg_ref[:, cols])
        b = _dot(h, wu_ref[:, cols])
        part = _dot((_silu(a) * b).astype(BF16), wd_ref[cols, :])
        if f == 0:
            acc_scr[...] = part
        else:
            acc_scr[...] += part
    y = x + 0.5 * gt_ref[...] * acc_scr[...].reshape(x.shape)
    if final_norm:
        y = y * lax.rsqrt(jnp.mean(y * y, axis=-1, keepdims=True) + NORM_EPS) * gfin_ref[...]
    o_ref[...] = y


def _ffn(x, shift, scale, gate, gain, wg, wu, wd, gfin=None, rows_target=512):
    g, r, d = x.shape
    dff = wg.shape[1]
    assert dff % MXU_WIDTH == 0
    gb, rb = _row_blocks(g, r, rows_target)
    rows = gb * rb
    mod_spec = pl.BlockSpec((gb, 1, d), lambda i, j: (i, 0, 0))
    vec_spec = pl.BlockSpec((1, d), lambda i, j: (0, 0))
    resident = lambda a: pl.BlockSpec(a.shape, lambda i, j: (0, 0), pipeline_mode=pl.Buffered(1))
    in_specs = [pl.BlockSpec((gb, rb, d), lambda i, j: (i, j, 0)),
                mod_spec, mod_spec, mod_spec, vec_spec,
                resident(wg), resident(wu), resident(wd)]
    args = [x, shift, scale, gate, gain.reshape(1, d), wg, wu, wd]
    if gfin is not None:
        in_specs.append(vec_spec)
        args.append(gfin.reshape(1, d))
    return pl.pallas_call(
        functools.partial(_ffn_kernel, final_norm=gfin is not None),
        grid=(g // gb, r // rb),
        in_specs=in_specs,
        out_specs=pl.BlockSpec((gb, rb, d), lambda i, j: (i, j, 0)),
        out_shape=jax.ShapeDtypeStruct(x.shape, F32),
        scratch_shapes=[pltpu.VMEM((rows, d), F32)],
        compiler_params=_params(("arbitrary", "arbitrary")),
        name="ffn",
    )(*args)


PROJ_COLS = (DN_CONV_DIM, DN_WIDTH, SB_WIDTH, LANES)


def _proj_kernel(x_ref, sh_ref, sc_ref, gain_ref, w_ref, wk_ref, wv_ref, *o_refs, transposed):
    rows = o_refs[0].shape[0]
    h = _modnorm(x_ref[...], gain_ref[...], sh_ref[...], sc_ref[...])
    h = h.reshape(rows, h.shape[-1]).astype(BF16)
    off = 0
    for o_ref, cols in zip(o_refs[0:len(PROJ_COLS)], PROJ_COLS):
        o_ref[...] = _dot(h, w_ref[:, off:off + cols])
        off += cols
    if transposed:
        k_ref, v_ref, kb_ref, vb_ref = o_refs[len(PROJ_COLS):]
        kt = _dot(wk_ref[...], h, NT)
        vt = _dot(wv_ref[...], h, NT)
        k_ref[...] = kt
        v_ref[...] = vt
        kb_ref[...] = kt.astype(BF16)
        vb_ref[...] = vt.astype(BF16)
    else:
        k_ref, v_ref = o_refs[len(PROJ_COLS):]
        k_ref[...] = _dot(h, wk_ref[...])
        v_ref[...] = _dot(h, wv_ref[...])


def _proj(x, shift, scale, gain, w_r, w_k, w_v, transposed, rows_target=512):
    g, r, d = x.shape
    gb, rb = _row_blocks(g, r, rows_target)
    rows = gb * rb
    nj = r // rb
    n = g * r
    mod_spec = pl.BlockSpec((gb, 1, d), lambda i, j: (i, 0, 0))
    const = lambda a: pl.BlockSpec(a.shape, lambda i, j: (0, 0))
    out_specs = [pl.BlockSpec((rows, c), lambda i, j: (i * nj + j, 0)) for c in PROJ_COLS]
    out_shape = [jax.ShapeDtypeStruct((n, c), F32) for c in PROJ_COLS]
    if transposed:
        out_specs += [pl.BlockSpec((SB_WIDTH, rows), lambda i, j: (0, i * nj + j))] * 4
        out_shape += [jax.ShapeDtypeStruct((SB_WIDTH, n), dt) for dt in (F32, F32, BF16, BF16)]
    else:
        out_specs += [pl.BlockSpec((rows, SB_WIDTH), lambda i, j: (i * nj + j, 0))] * 2
        out_shape += [jax.ShapeDtypeStruct((n, SB_WIDTH), F32)] * 2
    return pl.pallas_call(
        functools.partial(_proj_kernel, transposed=transposed),
        grid=(g // gb, nj),
        in_specs=[pl.BlockSpec((gb, rb, d), lambda i, j: (i, j, 0)),
                  mod_spec, mod_spec,
                  pl.BlockSpec((1, d), lambda i, j: (0, 0)),
                  const(w_r), const(w_k), const(w_v)],
        out_specs=out_specs,
        out_shape=out_shape,
        compiler_params=_params(("arbitrary", "arbitrary")),
        name="proj_in",
    )(x, shift, scale, gain.reshape(1, d), w_r, w_k, w_v)


def _outproj_kernel(x_ref, gt_ref, odn_ref, osb_ref, w_ref, o_ref):
    x = x_ref[...]
    mixed = (_dot(odn_ref[...].astype(BF16), w_ref[0:DN_WIDTH, :])
             + _dot(osb_ref[...].astype(BF16), w_ref[DN_WIDTH:DN_WIDTH + SB_WIDTH, :]))
    o_ref[...] = x + gt_ref[...] * mixed.reshape(x.shape)


def _outproj(x, gate, o_dn, o_sb, w_out, rows_target=512):
    g, r, d = x.shape
    gb, rb = _row_blocks(g, r, rows_target)
    rows = gb * rb
    nj = r // rb
    return pl.pallas_call(
        _outproj_kernel,
        grid=(g // gb, nj),
        in_specs=[pl.BlockSpec((gb, rb, d), lambda i, j: (i, j, 0)),
                  pl.BlockSpec((gb, 1, d), lambda i, j: (i, 0, 0)),
                  pl.BlockSpec((rows, DN_WIDTH), lambda i, j: (i * nj + j, 0)),
                  pl.BlockSpec((rows, SB_WIDTH), lambda i, j: (i * nj + j, 0)),
                  pl.BlockSpec(w_out.shape, lambda i, j: (0, 0))],
        out_specs=pl.BlockSpec((gb, rb, d), lambda i, j: (i, j, 0)),
        out_shape=jax.ShapeDtypeStruct(x.shape, F32),
        compiler_params=_params(("arbitrary", "arbitrary")),
        name="proj_out",
    )(x, gate, o_dn, o_sb, w_out)


def _dn_masks(group):
    c = DN_CHUNK
    i = jnp.arange(c)[:, None]
    m = jnp.arange(c)[None, :]
    same = (i // group) == (m // group)
    lower = jnp.logical_and(m <= i, same)
    lt = jnp.concatenate([lower, same], axis=0).astype(BF16)
    sext = jnp.concatenate([(i > m), jnp.ones((c, c), bool)], axis=1).astype(F32)
    return lt, sext


def _dn_gates(ab, alog, dtb):
    g = -jnp.exp(alog) * _softplus(ab + dtb)
    beta = _sigmoid(ab)
    return g, beta


def _dn_local(items, lt, sext, group):
    c = DN_CHUNK
    row = lax.broadcasted_iota(jnp.int32, (c, c), 0)
    col = lax.broadcasted_iota(jnp.int32, (c, c), 1)
    lower = col <= row
    strict = col < row
    if group < c:
        shift = group.bit_length() - 1
        same = (row >> shift) == (col >> shift)
        lower = jnp.logical_and(lower, same)
        strict = jnp.logical_and(strict, same)
    eye = (row == col).astype(F32)
    n_sq = max(1, (min(group, c) - 1).bit_length() - 1)

    qs, ks, kbs, dms, gams, tots = [], [], [], [], [], []
    for q_raw, k_raw, _, g_col, beta_col in items:
        qs.append(q_raw * lax.rsqrt(jnp.sum(q_raw * q_raw, axis=-1, keepdims=True) + L2_EPS)
                  * (DN_HEAD_DIM ** -0.5))
        k = k_raw * lax.rsqrt(jnp.sum(k_raw * k_raw, axis=-1, keepdims=True) + L2_EPS)
        ks.append(k)
        kbs.append(k * beta_col)
    for _, _, _, g_col, _ in items:
        r = _mm_mask_left(lt, g_col * sext)
        diff = r[0:c, 0:c]
        gams.append(r[0:c, c:c + 1])
        tots.append(r[c:2 * c, c:c + 1])
        dms.append(jnp.where(lower, jnp.exp(jnp.where(lower, diff, 0.0)), 0.0))
    a_s = [jnp.where(strict, _mm3(kb, k, NT) * dm, 0.0) for kb, k, dm in zip(kbs, ks, dms)]
    attns = [_mm3(q, k, NT) * dm for q, k, dm in zip(qs, ks, dms)]
    ps = [eye - a for a in a_s]
    xs = a_s
    for _ in range(n_sq):
        xs = [_mm3(x, x) for x in xs]
        ps = [p + _mm3(p, x) for p, x in zip(ps, xs)]
    egs = [jnp.exp(gam) for gam in gams]
    uws = [_mm3(p, jnp.concatenate([it[2] * it[4], kb * eg], axis=1))
           for p, it, kb, eg in zip(ps, items, kbs, egs)]
    out = []
    for q, k, uw, attn, eg, gam, tot in zip(qs, ks, uws, attns, egs, gams, tots):
        out.append((uw[:, 0:DN_HEAD_DIM], uw[:, DN_HEAD_DIM:2 * DN_HEAD_DIM], attn,
                    q * eg, k * jnp.exp(tot - gam), tot))
    return out


def _dn_items(qkv, g_all, beta_all, r0):
    c = DN_CHUNK
    dh = DN_HEAD_DIM
    items = []
    for h in range(DN_HEADS):
        items.append((qkv[r0:r0 + c, h * dh:(h + 1) * dh],
                      qkv[r0:r0 + c, DN_WIDTH + h * dh:DN_WIDTH + (h + 1) * dh],
                      qkv[r0:r0 + c, 2 * DN_WIDTH + h * dh:2 * DN_WIDTH + (h + 1) * dh],
                      g_all[r0:r0 + c, h:h + 1],
                      beta_all[r0:r0 + c, DN_HEADS + h:DN_HEADS + h + 1]))
    return items


def _dn_out(o, z, ng):
    y = o * lax.rsqrt(jnp.mean(o * o, axis=-1, keepdims=True) + NORM_EPS) * ng
    return y * _silu(z)


def _dn_prompt_kernel(qkv_ref, z_ref, ab_ref, cw_ref, alog_ref, dtb_ref, ng_ref, lt_ref, sext_ref,
                      o_ref, s_out_ref, conv_out_ref, xp_scr, s_scr):
    c = DN_CHUNK
    dh = DN_HEAD_DIM
    tb = qkv_ref.shape[0]
    step = pl.program_id(0)

    @pl.when(step == 0)
    def _():
        xp_scr[0:8, :] = jnp.zeros((8, DN_CONV_DIM), F32)
        s_scr[...] = jnp.zeros_like(s_scr)

    x = qkv_ref[...]
    xp_scr[8:8 + tb, :] = x
    cw = cw_ref[...]
    y = xp_scr[5:5 + tb, :] * cw[0:1, :]
    y = y + xp_scr[6:6 + tb, :] * cw[1:2, :]
    y = y + xp_scr[7:7 + tb, :] * cw[2:3, :]
    y = y + x * cw[3:4, :]
    conv_out_ref[...] = xp_scr[5 + tb:8 + tb, :]
    xp_scr[0:8, :] = xp_scr[tb:tb + 8, :]
    qkv = _silu(y)

    g_all, beta_all = _dn_gates(ab_ref[...], alog_ref[...], dtb_ref[...])
    ng = ng_ref[...]
    items = []
    for n in range(tb // c):
        items += _dn_items(qkv, g_all, beta_all, n * c)
    local = _dn_local(items, lt_ref[...], sext_ref[...], c)
    s = [s_scr[h] for h in range(DN_HEADS)]
    for n in range(tb // c):
        loc = local[n * DN_HEADS:(n + 1) * DN_HEADS]
        v_new = [u - _mm3(w, s[h]) for h, (u, w, _, _, _, _) in enumerate(loc)]
        o = [_mm3(q_dec, s[h]) + _mm3(attn, v_new[h]) for h, (_, _, attn, q_dec, _, _) in enumerate(loc)]
        s = [s[h] * jnp.exp(tot[0:1, :]) + _mm3(k_dec, v_new[h], TN)
             for h, (_, _, _, _, k_dec, tot) in enumerate(loc)]
        for h in range(DN_HEADS):
            o_ref[n * c:(n + 1) * c, h * dh:(h + 1) * dh] = _dn_out(
                o[h], z_ref[n * c:(n + 1) * c, h * dh:(h + 1) * dh], ng)
    for h in range(DN_HEADS):
        s_scr[h] = s[h]
        s_out_ref[h] = s[h]


def _dn_prompt(qkv, z, ab, conv_w, alog, dtb, ng, chunks_per_step=4):
    t = qkv.shape[0]
    c = DN_CHUNK * min(chunks_per_step, t // DN_CHUNK)
    assert t % c == 0
    lt, sext = _dn_masks(DN_CHUNK)
    const = lambda shape: pl.BlockSpec(shape, lambda i: tuple(0 for _ in shape))
    return pl.pallas_call(
        _dn_prompt_kernel,
        grid=(t // c,),
        in_specs=[pl.BlockSpec((c, DN_CONV_DIM), lambda i: (i, 0)),
                  pl.BlockSpec((c, DN_WIDTH), lambda i: (i, 0)),
                  pl.BlockSpec((c, LANES), lambda i: (i, 0)),
                  const((DN_CONV_W, DN_CONV_DIM)), const((1, LANES)), const((1, LANES)),
                  const((1, DN_HEAD_DIM)), const(lt.shape), const(sext.shape)],
        out_specs=[pl.BlockSpec((c, DN_WIDTH), lambda i: (i, 0)),
                   const((DN_HEADS, DN_HEAD_DIM, DN_HEAD_DIM)),
                   const((DN_CONV_W - 1, DN_CONV_DIM))],
        out_shape=[jax.ShapeDtypeStruct((t, DN_WIDTH), F32),
                   jax.ShapeDtypeStruct((DN_HEADS, DN_HEAD_DIM, DN_HEAD_DIM), F32),
                   jax.ShapeDtypeStruct((DN_CONV_W - 1, DN_CONV_DIM), F32)],
        scratch_shapes=[pltpu.VMEM((c + 8, DN_CONV_DIM), F32),
                        pltpu.VMEM((DN_HEADS, DN_HEAD_DIM, DN_HEAD_DIM), F32)],
        compiler_params=_params(("arbitrary",)),
        name="deltanet_prompt",
    )(qkv, z, ab, conv_w, alog, dtb, ng, lt, sext)


def _dn_sample_kernel(qkv_ref, z_ref, ab_ref, cprev_ref, s_ref, cw_ref, alog_ref, dtb_ref, ng_ref,
                      lt_ref, sext_ref, o_ref, s_out_ref, conv_out_ref, xp_scr, *, t):
    c = DN_CHUNK
    dh = DN_HEAD_DIM
    nreq = c // t
    x3 = qkv_ref[...]
    xp_scr[:, 8 - (DN_CONV_W - 1):8, :] = cprev_ref[...]
    xp_scr[:, 8:8 + t, :] = x3
    cw = cw_ref[...]
    y = xp_scr[:, 5:5 + t, :] * cw[0:1, :]
    y = y + xp_scr[:, 6:6 + t, :] * cw[1:2, :]
    y = y + xp_scr[:, 7:7 + t, :] * cw[2:3, :]
    y = y + x3 * cw[3:4, :]
    conv_out_ref[...] = xp_scr[:, 5 + t:8 + t, :]
    qkv = _silu(y).reshape(c, DN_CONV_DIM)

    g_all, beta_all = _dn_gates(ab_ref[...], alog_ref[...], dtb_ref[...])
    ng = ng_ref[...]
    rowid = lax.broadcasted_iota(jnp.int32, (c, 1), 0)
    local = _dn_local(_dn_items(qkv, g_all, beta_all, 0), lt_ref[...], sext_ref[...], t)
    for h in range(DN_HEADS):
        u, w, attn, q_dec, k_dec, tot = local[h]
        ws, qs = [], []
        for r in range(nreq):
            s = s_ref[r, h]
            ws.append(_mm3(w, s)[r * t:(r + 1) * t, :])
            qs.append(_mm3(q_dec, s)[r * t:(r + 1) * t, :])
        v_new = u - jnp.concatenate(ws, axis=0)
        o = jnp.concatenate(qs, axis=0) + _mm3(attn, v_new)
        for r in range(nreq):
            in_req = jnp.logical_and(rowid >= r * t, rowid < (r + 1) * t)
            kd_r = jnp.where(in_req, k_dec, 0.0)
            s_out_ref[r, h] = (s_ref[r, h] * jnp.exp(tot[r * t:r * t + 1, :])
                               + _mm3(kd_r, v_new, TN))
        o_ref[:, h * dh:(h + 1) * dh] = _dn_out(o, z_ref[:, h * dh:(h + 1) * dh], ng)


def _dn_sample(qkv, z, ab, conv_prev, ssm_prev, conv_w, alog, dtb, ng, t):
    n = qkv.shape[0]
    b = n // t
    c = DN_CHUNK
    assert t == 8 and c % t == 0 and b % (c // t) == 0
    nreq = c // t
    lt, sext = _dn_masks(t)
    const = lambda shape: pl.BlockSpec(shape, lambda i: tuple(0 for _ in shape))
    return pl.pallas_call(
        functools.partial(_dn_sample_kernel, t=t),
        grid=(b // nreq,),
        in_specs=[pl.BlockSpec((nreq, t, DN_CONV_DIM), lambda i: (i, 0, 0)),
                  pl.BlockSpec((c, DN_WIDTH), lambda i: (i, 0)),
                  pl.BlockSpec((c, LANES), lambda i: (i, 0)),
                  pl.BlockSpec((nreq, DN_CONV_W - 1, DN_CONV_DIM), lambda i: (i, 0, 0)),
                  pl.BlockSpec((nreq, DN_HEADS, DN_HEAD_DIM, DN_HEAD_DIM), lambda i: (i, 0, 0, 0)),
                  const((DN_CONV_W, DN_CONV_DIM)), const((1, LANES)), const((1, LANES)),
                  const((1, DN_HEAD_DIM)), const(lt.shape), const(sext.shape)],
        out_specs=[pl.BlockSpec((c, DN_WIDTH), lambda i: (i, 0)),
                   pl.BlockSpec((nreq, DN_HEADS, DN_HEAD_DIM, DN_HEAD_DIM), lambda i: (i, 0, 0, 0)),
                   pl.BlockSpec((nreq, DN_CONV_W - 1, DN_CONV_DIM), lambda i: (i, 0, 0))],
        out_shape=[jax.ShapeDtypeStruct((n, DN_WIDTH), F32),
                   jax.ShapeDtypeStruct(ssm_prev.shape, F32),
                   jax.ShapeDtypeStruct(conv_prev.shape, F32)],
        scratch_shapes=[pltpu.VMEM((nreq, 8 + t, DN_CONV_DIM), F32)],
        compiler_params=_params(("arbitrary",)),
        name="deltanet_sample",
    )(qkv.reshape(b, t, DN_CONV_DIM), z, ab, conv_prev, ssm_prev, conv_w, alog, dtb, ng, lt, sext)


LOG2E = 1.4426950408889634


def _sb_softplus(z2, vis):
    sp = jnp.maximum(z2, 0.0) + jnp.log2(1.0 + jnp.exp2(-jnp.abs(z2)))
    if vis is not None:
        sp = jnp.where(vis, sp, 0.0)
    return sp.astype(BF16)


def _sb_weights(z2, cum, carry, vis):
    w = jnp.exp2(z2 + cum + carry)
    if vis is not None:
        w = jnp.where(vis, w, 0.0)
    return w.astype(BF16)


def _neg_upper(tk):
    j = jnp.arange(tk)[:, None]
    s = jnp.arange(tk)[None, :]
    return -(j >= s).astype(BF16)


def _sbp_kernel(bias_ref, q_ref, k_ref, v_ref, u_ref, g_ref, o_ref, acc_scr, carry_scr,
                z0_scr, z1_scr, w0_scr, w1_scr, *, tq, tk):
    z_scr = (z0_scr, z1_scr)
    w_scr = (w0_scr, w1_scr)
    p = pl.program_id(0)
    i = pl.program_id(1)
    nsub = tq // tk
    chains = [(hh, r) for hh in range(2) for r in range(nsub)]
    nch = len(chains)
    n_tiles = nsub * (i + 1)
    lane = lax.broadcasted_iota(jnp.int32, (1, LANES), 1)
    head_lanes = [lane < SB_HEAD_DIM, lane >= SB_HEAD_DIM]
    qs = q_ref[...] * (SB_HEAD_DIM ** -0.5 * LOG2E)
    qh = {(hh, r): jnp.where(head_lanes[hh], qs[r * tk:(r + 1) * tk, :], 0.0).astype(BF16)
          for hh, r in chains}
    bias = [bias_ref[2 * p] * LOG2E, bias_ref[2 * p + 1] * LOG2E]
    uneg = u_ref[...]
    col_minus_row = (lax.broadcasted_iota(jnp.int32, (tk, tk), 1)
                     - lax.broadcasted_iota(jnp.int32, (tk, tk), 0))
    acc_scr[...] = jnp.zeros_like(acc_scr)
    carry_scr[...] = jnp.zeros_like(carry_scr)

    def key_tile(n):
        j = jnp.maximum(n_tiles - 1 - n, 0)
        return j, pl.multiple_of(j * tk, tk)

    def logits(n, slot, ci):
        _, start = key_tile(n)
        kt = k_ref[:, pl.ds(start, tk)]
        z_scr[slot][ci] = _dot(qh[chains[ci]], kt) + bias[chains[ci][0]]

    def values(n, slot, ci):
        _, start = key_tile(n)
        vt = v_ref[:, pl.ds(start, tk)]
        acc_scr[ci] += _dot(w_scr[slot][ci], vt, NT)

    def visit(n, slot, masked, with_values):
        j, _ = key_tile(n)
        if masked:
            vis = [col_minus_row < (i * tq + r * tk - j * tk) for _, r in chains]
        else:
            vis = [None] * nch
        zs, sps, cums = [], [], []
        for ci in range(nch):
            zs.append(z_scr[slot][ci])
            sps.append(_sb_softplus(zs[ci], vis[ci]))
            logits(n + 1, 1 - slot, ci)
        for ci in range(nch):
            cums.append(_dot(sps[ci], uneg))
            if with_values:
                values(n - 1, 1 - slot, ci)
        for ci in range(nch):
            w_scr[slot][ci] = _sb_weights(zs[ci], cums[ci], carry_scr[ci], vis[ci])
            carry_scr[ci] += cums[ci][:, 0:1]

    for ci in range(nch):
        logits(0, 0, ci)
    for n in range(nsub):
        visit(n, n % 2, True, n > 0)

    def body(m, carry):
        n = nsub + 2 * m
        for d in range(2):
            visit(n + d, (nsub + d) % 2, False, True)
        return carry

    lax.fori_loop(0, (n_tiles - nsub) // 2, body, 0)
    for ci in range(nch):
        values(n_tiles - 1, (nsub - 1) % 2, ci)
    for r in range(nsub):
        out = jnp.where(head_lanes[0], acc_scr[chains.index((0, r))], acc_scr[chains.index((1, r))])
        sq = out * out
        ms = [jnp.sum(jnp.where(m, sq, 0.0), axis=-1, keepdims=True) * (1.0 / SB_HEAD_DIM)
              for m in head_lanes]
        ms = jnp.where(head_lanes[0], ms[0], ms[1])
        o_ref[r * tk:(r + 1) * tk, :] = out * lax.rsqrt(ms + NORM_EPS) * g_ref[...]


def _sb_prompt(q, kt_bf, vt_bf, bias, g2, tq=512, tk=256):
    t = q.shape[0]
    tq = min(tq, t)
    assert t % tq == 0 and tq % (2 * tk) == 0
    npairs = SB_WIDTH // LANES
    nchains = 2 * (tq // tk)
    return pl.pallas_call(
        functools.partial(_sbp_kernel, tq=tq, tk=tk),
        grid=(npairs, t // tq),
        in_specs=[pl.BlockSpec(memory_space=pltpu.SMEM),
                  pl.BlockSpec((tq, LANES), lambda p, i: (i, p)),
                  pl.BlockSpec((LANES, t), lambda p, i: (p, 0)),
                  pl.BlockSpec((LANES, t), lambda p, i: (p, 0)),
                  pl.BlockSpec((tk, tk), lambda p, i: (0, 0)),
                  pl.BlockSpec((1, LANES), lambda p, i: (0, 0))],
        out_specs=pl.BlockSpec((tq, LANES), lambda p, i: (i, p)),
        out_shape=jax.ShapeDtypeStruct((t, SB_WIDTH), F32),
        scratch_shapes=[pltpu.VMEM((nchains, tk, LANES), F32),
                        pltpu.VMEM((nchains, tk, 1), F32),
                        pltpu.VMEM((nchains, tk, tk), F32), pltpu.VMEM((nchains, tk, tk), F32),
                        pltpu.VMEM((nchains, tk, tk), BF16), pltpu.VMEM((nchains, tk, tk), BF16)],
        compiler_params=_params(("arbitrary", "arbitrary")),
        name="sb_prompt",
    )(bias, q, kt_bf, vt_bf, _neg_upper(tk), g2)


SB_PAGES_PER_STEP = 16


def _sbs_kernel(pt_ref, q_ref, kn_ref, vn_ref, brow_ref, u_ref, g_ref, *rest, t, page):
    npg = SB_PAGES_PER_STEP
    k_refs = rest[0:npg]
    v_refs = rest[npg:2 * npg]
    o_ref, qbd_scr, acc_scr, carry_scr = rest[2 * npg:]
    del pt_ref
    g = pl.program_id(1)
    rows = SB_HEADS * t
    t_shift = t.bit_length() - 1
    head_shift = SB_HEAD_DIM.bit_length() - 1
    uneg = u_ref[...]
    brow = brow_ref[...] * LOG2E

    @pl.when(g == 0)
    def _():
        row = lax.broadcasted_iota(jnp.int32, (rows, SB_WIDTH), 0)
        col = lax.broadcasted_iota(jnp.int32, (rows, SB_WIDTH), 1)
        q8 = q_ref[0] * (SB_HEAD_DIM ** -0.5 * LOG2E)
        qt = jnp.concatenate([q8] * SB_HEADS, axis=0)
        qbd_scr[...] = jnp.where((row >> t_shift) == (col >> head_shift), qt, 0.0).astype(BF16)
        pad = jnp.zeros((page - t, SB_WIDTH), F32)
        kn = jnp.concatenate([kn_ref[0], pad], axis=0).astype(BF16)
        vn = jnp.concatenate([vn_ref[0], pad], axis=0).astype(BF16)
        krow = lax.broadcasted_iota(jnp.int32, (rows, page), 0)
        kcol = lax.broadcasted_iota(jnp.int32, (rows, page), 1)
        vis = kcol < (krow & (t - 1))
        z = _dot(qbd_scr[...], kn, NT) + brow
        cum = _dot(_sb_softplus(z, vis), uneg)
        acc_scr[...] = _dot(_sb_weights(z, cum, 0.0, vis), vn)
        carry_scr[...] = cum[:, 0:1]

    qbd = qbd_scr[...]
    zs = [_dot(qbd, k_refs[i][0].astype(BF16)) + brow for i in range(npg)]
    cums = [_dot(_sb_softplus(z, None), uneg) for z in zs]
    carry = carry_scr[...]
    acc = acc_scr[...]
    for i in range(npg):
        w = _sb_weights(zs[i], cums[i], carry, None)
        acc = acc + _dot(w, v_refs[i][0].astype(BF16), NT)
        carry = carry + cums[i][:, 0:1]
    acc_scr[...] = acc
    carry_scr[...] = carry

    @pl.when(g == pl.num_programs(1) - 1)
    def _():
        acc = acc_scr[...]
        col = lax.broadcasted_iota(jnp.int32, (t, SB_WIDTH), 1)
        out = jnp.zeros((t, SB_WIDTH), F32)
        for h in range(SB_HEADS):
            out = out + jnp.where((col >> head_shift) == h, acc[h * t:(h + 1) * t, :], 0.0)
        sq = out * out
        ms = jnp.zeros((t, SB_WIDTH), F32)
        for h in range(SB_HEADS):
            in_h = (col >> head_shift) == h
            ms_h = jnp.sum(jnp.where(in_h, sq, 0.0), axis=-1, keepdims=True) * (1.0 / SB_HEAD_DIM)
            ms = jnp.where(in_h, ms_h, ms)
        o_ref[0] = out * lax.rsqrt(ms + NORM_EPS) * g_ref[...]


def _sb_sample(q, k_new, v_new, pool_k, pool_v, page_table, bias, g8):
    b, t, _ = q.shape
    n_pages = page_table.shape[1]
    page = pool_k.shape[2]
    npg = SB_PAGES_PER_STEP
    assert n_pages % npg == 0 and page == LANES and t == 8
    rows = SB_HEADS * t
    brow = jnp.broadcast_to(jnp.repeat(bias, t)[:, None], (rows, LANES))
    u01 = _neg_upper(page)
    pt_flat = page_table.reshape(-1)

    def page_spec(i):
        return pl.BlockSpec(
            (1, SB_WIDTH, page),
            lambda r, g, pt: (pt[r * n_pages + (n_pages - 1 - (g * npg + i))], 0, 0))

    req_spec = pl.BlockSpec((1, t, SB_WIDTH), lambda r, g, pt: (r, 0, 0))
    grid_spec = pltpu.PrefetchScalarGridSpec(
        num_scalar_prefetch=1,
        grid=(b, n_pages // npg),
        in_specs=[req_spec, req_spec, req_spec,
                  pl.BlockSpec((rows, LANES), lambda r, g, pt: (0, 0)),
                  pl.BlockSpec((page, page), lambda r, g, pt: (0, 0)),
                  pl.BlockSpec((1, SB_WIDTH), lambda r, g, pt: (0, 0))]
                 + [page_spec(i) for i in range(npg)] + [page_spec(i) for i in range(npg)],
        out_specs=req_spec,
        scratch_shapes=[pltpu.VMEM((rows, SB_WIDTH), BF16),
                        pltpu.VMEM((rows, SB_WIDTH), F32),
                        pltpu.VMEM((rows, 1), F32)])
    return pl.pallas_call(
        functools.partial(_sbs_kernel, t=t, page=page),
        grid_spec=grid_spec,
        out_shape=jax.ShapeDtypeStruct((b, t, SB_WIDTH), F32),
        compiler_params=_params(("arbitrary", "arbitrary")),
        name="sb_sample",
    )(pt_flat, q, k_new, v_new, brow, u01, g8, *([pool_k] * npg), *([pool_v] * npg))


def _split_w_in(w_in):
    d = w_in.shape[0]
    o_ab = DN_CONV_DIM + DN_WIDTH
    o_q = o_ab + 2 * DN_HEADS
    o_k = o_q + SB_WIDTH
    o_v = o_k + SB_WIDTH
    pad = jnp.zeros((d, LANES - 2 * DN_HEADS), w_in.dtype)
    w_r = jnp.concatenate([w_in[:, 0:o_ab], w_in[:, o_q:o_k], w_in[:, o_ab:o_q], pad], axis=1)
    return w_r.astype(BF16), w_in[:, o_k:o_v].astype(BF16), w_in[:, o_v:].astype(BF16)


def _heads_last(xt, n):
    return jnp.transpose(xt.reshape(SB_HEADS, SB_HEAD_DIM, n), (2, 0, 1))


def _lane_row(v):
    return jnp.pad(v.astype(F32), (0, LANES - v.shape[0])).reshape(1, LANES)


def kernel(x_prompt, x_sample, c_prompt, c_sample, cache_sb_k, cache_sb_v, page_table, state_dn_ssm, state_dn_conv, w_ada, b_ada, g_ffn1, w1_gate, w1_up, w1_down, g_mix, w_in, dn_conv_w, dn_a_log, dn_dt_bias, dn_norm_g, sb_norm_g, sb_bias, w_out, g_ffn2, w2_gate, w2_up, w2_down, g_final):
    depth = w_ada.shape[0]
    bsz, seq, d = x_prompt.shape
    dec_b, dec_t, _ = x_sample.shape
    assert bsz == 1
    xp, xs = x_prompt, x_sample
    c_all = jnp.concatenate([c_prompt, c_sample], axis=0)
    n_c = c_all.shape[0]
    c_all = jnp.pad(c_all, ((0, (-n_c) % 8), (0, 0)))
    outs = {name: [] for name in ("kp", "vp", "ssmp", "convp", "ks", "vs", "ssms", "convs")}
    for l in range(depth):
        last = l == depth - 1
        mods = _adaln(c_all, w_ada[l].astype(BF16), b_ada[l]).reshape(-1, N_MOD, d)
        mp = [mods[0:bsz, i:i + 1, :] for i in range(N_MOD)]
        ms = [mods[bsz:bsz + dec_b, i:i + 1, :] for i in range(N_MOD)]
        w1 = (w1_gate[l].astype(BF16), w1_up[l].astype(BF16), w1_down[l].astype(BF16))
        w2 = (w2_gate[l].astype(BF16), w2_up[l].astype(BF16), w2_down[l].astype(BF16))
        w_in_r, w_k, w_v = _split_w_in(w_in[l])
        w_out_b = w_out[l].astype(BF16)
        alog = _lane_row(dn_a_log[l])
        dtb = _lane_row(dn_dt_bias[l])
        ng = dn_norm_g[l].reshape(1, DN_HEAD_DIM).astype(F32)
        g_sb = sb_norm_g[l].astype(F32)
        g2 = jnp.tile(g_sb, LANES // SB_HEAD_DIM).reshape(1, LANES)
        g8 = jnp.tile(g_sb, SB_HEADS).reshape(1, SB_WIDTH)
        bias = sb_bias[l].astype(F32)

        xp = _ffn(xp, mp[0], mp[1], mp[2], g_ffn1[l], *w1)
        qkv, z, q, ab, kt, vt, kt_bf, vt_bf = _proj(xp, mp[3], mp[4], g_mix[l], w_in_r, w_k.T, w_v.T, True)
        o_dn, ssm_p, conv_p = _dn_prompt(qkv, z, ab, dn_conv_w[l], alog, dtb, ng)
        o_sb = _sb_prompt(q, kt_bf, vt_bf, bias, g2)
        xp = _outproj(xp, mp[5], o_dn, o_sb, w_out_b)
        xp = _ffn(xp, mp[6], mp[7], mp[8], g_ffn2[l], *w2, gfin=g_final if last else None)
        outs["kp"].append(_heads_last(kt, seq).reshape(bsz, seq, SB_HEADS, SB_HEAD_DIM))
        outs["vp"].append(_heads_last(vt, seq).reshape(bsz, seq, SB_HEADS, SB_HEAD_DIM))
        outs["ssmp"].append(ssm_p.reshape(bsz, DN_HEADS, DN_HEAD_DIM, DN_HEAD_DIM))
        outs["convp"].append(conv_p.reshape(bsz, DN_CONV_W - 1, DN_CONV_DIM))

        xs = _ffn(xs, ms[0], ms[1], ms[2], g_ffn1[l], *w1)
        qkv, z, q, ab, k, v = _proj(xs, ms[3], ms[4], g_mix[l], w_in_r, w_k, w_v, False)
        o_dn, ssm_s, conv_s = _dn_sample(qkv, z, ab, state_dn_conv[l], state_dn_ssm[l],
                                         dn_conv_w[l], alog, dtb, ng, dec_t)
        n_pool, page = cache_sb_k.shape[1], cache_sb_k.shape[2]
        pool_k = jnp.transpose(cache_sb_k[l], (0, 2, 3, 1)).reshape(n_pool, SB_WIDTH, page)
        pool_v = jnp.transpose(cache_sb_v[l], (0, 2, 3, 1)).reshape(n_pool, SB_WIDTH, page)
        o_sb = _sb_sample(q.reshape(dec_b, dec_t, SB_WIDTH), k.reshape(dec_b, dec_t, SB_WIDTH),
                          v.reshape(dec_b, dec_t, SB_WIDTH), pool_k, pool_v, page_table, bias, g8)
        xs = _outproj(xs, ms[5], o_dn, o_sb.reshape(dec_b * dec_t, SB_WIDTH), w_out_b)
        xs = _ffn(xs, ms[6], ms[7], ms[8], g_ffn2[l], *w2, gfin=g_final if last else None)
        outs["ks"].append(k.reshape(dec_b, dec_t, SB_HEADS, SB_HEAD_DIM))
        outs["vs"].append(v.reshape(dec_b, dec_t, SB_HEADS, SB_HEAD_DIM))
        outs["ssms"].append(ssm_s)
        outs["convs"].append(conv_s)
    if depth == 0:
        raise ValueError("depth must be positive")
    return (xp, xs,
            jnp.stack(outs["kp"]), jnp.stack(outs["vp"]), jnp.stack(outs["ssmp"]), jnp.stack(outs["convp"]),
            jnp.stack(outs["ks"]), jnp.stack(outs["vs"]), jnp.stack(outs["ssms"]), jnp.stack(outs["convs"]))
```

```python
import functools

import jax
import jax.numpy as jnp
from jax import lax
from jax.experimental import pallas as pl
from jax.experimental.pallas import tpu as pltpu

F32 = jnp.float32
BF16 = jnp.bfloat16

DN_HEADS = 4
DN_HEAD_DIM = 128
DN_WIDTH = DN_HEADS * DN_HEAD_DIM
DN_CONV_W = 4
DN_CONV_DIM = 3 * DN_WIDTH
DN_CHUNK = 64
SB_HEADS = 8
SB_HEAD_DIM = 64
SB_WIDTH = SB_HEADS * SB_HEAD_DIM
N_MOD = 9
NORM_EPS = 1e-6
L2_EPS = 1e-6
LANES = 128

NN = (((1,), (0,)), ((), ()))
NT = (((1,), (1,)), ((), ()))
TN = (((0,), (0,)), ((), ()))

VMEM_LIMIT = 56 * 1024 * 1024


def _params(sem):
    return pltpu.CompilerParams(dimension_semantics=sem, vmem_limit_bytes=VMEM_LIMIT)


def _dot(a, b, dims=NN):
    return lax.dot_general(a, b, dims, preferred_element_type=F32)


def _split2(x):
    hi = x.astype(BF16)
    lo = (x - hi.astype(F32)).astype(BF16)
    return hi, lo


def _split3(x):
    hi = x.astype(BF16)
    r = x - hi.astype(F32)
    mid = r.astype(BF16)
    lo = (r - mid.astype(F32)).astype(BF16)
    return hi, mid, lo


def _mm3(a, b, dims=NN):
    ah, al = _split2(a)
    bh, bl = _split2(b)
    return _dot(ah, bh, dims) + (_dot(ah, bl, dims) + _dot(al, bh, dims))


def _mm_mask_left(m01, x):
    hi, mid, lo = _split3(x)
    return _dot(m01, hi) + (_dot(m01, mid) + _dot(m01, lo))


def _sigmoid(x):
    return 1.0 / (1.0 + jnp.exp(-x))


def _silu(x):
    return x * _sigmoid(x)


def _softplus(x):
    return jnp.maximum(x, 0.0) + jnp.log1p(jnp.exp(-jnp.abs(x)))


def _modnorm(x, gain, shift, scale):
    y = x * lax.rsqrt(jnp.mean(x * x, axis=-1, keepdims=True) + NORM_EPS)
    return (y * gain) * (1.0 + scale) + shift


def _adaln_kernel(c_ref, w_ref, b_ref, o_ref):
    c = _silu(c_ref[...]).astype(BF16)
    o_ref[...] = _dot(c, w_ref[...]) + b_ref[...]


def _adaln(c, w_bf, b):
    n, d = c.shape
    cols = w_bf.shape[1]
    tn = d
    return pl.pallas_call(
        _adaln_kernel,
        grid=(cols // tn,),
        in_specs=[pl.BlockSpec((n, d), lambda j: (0, 0)),
                  pl.BlockSpec((d, tn), lambda j: (0, j)),
                  pl.BlockSpec((1, tn), lambda j: (0, j))],
        out_specs=pl.BlockSpec((n, tn), lambda j: (0, j)),
        out_shape=jax.ShapeDtypeStruct((n, cols), F32),
        compiler_params=_params(("arbitrary",)),
        name="adaln",
    )(c, w_bf, b.reshape(1, cols))


def _row_blocks(g, r, target):
    if r >= target:
        assert r % target == 0
        return 1, target
    gb = min(g, target // r)
    assert g % gb == 0
    return gb, r


MXU_WIDTH = 256


def _ffn_kernel(x_ref, sh_ref, sc_ref, gt_ref, gain_ref, wg_ref, wu_ref, wd_ref, *rest, final_norm):
    if final_norm:
        gfin_ref, o_ref, acc_scr = rest
    else:
        o_ref, acc_scr = rest
    rows, d = acc_scr.shape
    x = x_ref[...]
    h = _modnorm(x, gain_ref[...], sh_ref[...], sc_ref[...]).reshape(rows, d).astype(BF16)
    dff = wg_ref.shape[1]
    for f in range(dff // MXU_WIDTH):
        cols = slice(f * MXU_WIDTH, (f + 1) * MXU_WIDTH)
        a = _dot(h, wg_ref[:, cols])
        b = _dot(h, wu_ref[:, cols])
        part = _dot((_silu(a) * b).astype(BF16), wd_ref[cols, :])
        if f == 0:
            acc_scr[...] = part
        else:
            acc_scr[...] += part
    y = x + 0.5 * gt_ref[...] * acc_scr[...].reshape(x.shape)
    if final_norm:
        y = y * lax.rsqrt(jnp.mean(y * y, axis=-1, keepdims=True) + NORM_EPS) * gfin_ref[...]
    o_ref[...] = y


def _ffn(x, shift, scale, gate, gain, wg, wu, wd, gfin=None, rows_target=512):
    g, r, d = x.shape
    dff = wg.shape[1]
    assert dff % MXU_WIDTH == 0
    gb, rb = _row_blocks(g, r, rows_target)
    rows = gb * rb
    mod_spec = pl.BlockSpec((gb, 1, d), lambda i, j: (i, 0, 0))
    vec_spec = pl.BlockSpec((1, d), lambda i, j: (0, 0))
    resident = lambda a: pl.BlockSpec(a.shape, lambda i, j: (0, 0), pipeline_mode=pl.Buffered(1))
    in_specs = [pl.BlockSpec((gb, rb, d), lambda i, j: (i, j, 0)),
                mod_spec, mod_spec, mod_spec, vec_spec,
                resident(wg), resident(wu), resident(wd)]
    args = [x, shift, scale, gate, gain.reshape(1, d), wg, wu, wd]
    if gfin is not None:
        in_specs.append(vec_spec)
        args.append(gfin.reshape(1, d))
    return pl.pallas_call(
        functools.partial(_ffn_kernel, final_norm=gfin is not None),
        grid=(g // gb, r // rb),
        in_specs=in_specs,
        out_specs=pl.BlockSpec((gb, rb, d), lambda i, j: (i, j, 0)),
        out_shape=jax.ShapeDtypeStruct(x.shape, F32),
        scratch_shapes=[pltpu.VMEM((rows, d), F32)],
        compiler_params=_params(("arbitrary", "arbitrary")),
        name="ffn",
    )(*args)


PROJ_COLS = (DN_CONV_DIM, DN_WIDTH, SB_WIDTH, LANES)


def _proj_kernel(x_ref, sh_ref, sc_ref, gain_ref, w_ref, wk_ref, wv_ref, *o_refs, transposed):
    rows = o_refs[0].shape[0]
    h = _modnorm(x_ref[...], gain_ref[...], sh_ref[...], sc_ref[...])
    h = h.reshape(rows, h.shape[-1]).astype(BF16)
    off = 0
    for o_ref, cols in zip(o_refs[0:len(PROJ_COLS)], PROJ_COLS):
        o_ref[...] = _dot(h, w_ref[:, off:off + cols])
        off += cols
    if transposed:
        k_ref, v_ref, kb_ref, vb_ref = o_refs[len(PROJ_COLS):]
        kt = _dot(wk_ref[...], h, NT)
        vt = _dot(wv_ref[...], h, NT)
        k_ref[...] = kt
        v_ref[...] = vt
        kb_ref[...] = kt.astype(BF16)
        vb_ref[...] = vt.astype(BF16)
    else:
        k_ref, v_ref = o_refs[len(PROJ_COLS):]
        k_ref[...] = _dot(h, wk_ref[...])
        v_ref[...] = _dot(h, wv_ref[...])


def _proj(x, shift, scale, gain, w_r, w_k, w_v, transposed, rows_target=512):
    g, r, d = x.shape
    gb, rb = _row_blocks(g, r, rows_target)
    rows = gb * rb
    nj = r // rb
    n = g * r
    mod_spec = pl.BlockSpec((gb, 1, d), lambda i, j: (i, 0, 0))
    const = lambda a: pl.BlockSpec(a.shape, lambda i, j: (0, 0))
    out_specs = [pl.BlockSpec((rows, c), lambda i, j: (i * nj + j, 0)) for c in PROJ_COLS]
    out_shape = [jax.ShapeDtypeStruct((n, c), F32) for c in PROJ_COLS]
    if transposed:
        out_specs += [pl.BlockSpec((SB_WIDTH, rows), lambda i, j: (0, i * nj + j))] * 4
        out_shape += [jax.ShapeDtypeStruct((SB_WIDTH, n), dt) for dt in (F32, F32, BF16, BF16)]
    else:
        out_specs += [pl.BlockSpec((rows, SB_WIDTH), lambda i, j: (i * nj + j, 0))] * 2
        out_shape += [jax.ShapeDtypeStruct((n, SB_WIDTH), F32)] * 2
    return pl.pallas_call(
        functools.partial(_proj_kernel, transposed=transposed),
        grid=(g // gb, nj),
        in_specs=[pl.BlockSpec((gb, rb, d), lambda i, j: (i, j, 0)),
                  mod_spec, mod_spec,
                  pl.BlockSpec((1, d), lambda i, j: (0, 0)),
                  const(w_r), const(w_k), const(w_v)],
        out_specs=out_specs,
        out_shape=out_shape,
        compiler_params=_params(("arbitrary", "arbitrary")),
        name="proj_in",
    )(x, shift, scale, gain.reshape(1, d), w_r, w_k, w_v)


def _outproj_kernel(x_ref, gt_ref, odn_ref, osb_ref, w_ref, o_ref):
    x = x_ref[...]
    mixed = (_dot(odn_ref[...].astype(BF16), w_ref[0:DN_WIDTH, :])
             + _dot(osb_ref[...].astype(BF16), w_ref[DN_WIDTH:DN_WIDTH + SB_WIDTH, :]))
    o_ref[...] = x + gt_ref[...] * mixed.reshape(x.shape)


def _outproj(x, gate, o_dn, o_sb, w_out, rows_target=512):
    g, r, d = x.shape
    gb, rb = _row_blocks(g, r, rows_target)
    rows = gb * rb
    nj = r // rb
    return pl.pallas_call(
        _outproj_kernel,
        grid=(g // gb, nj),
        in_specs=[pl.BlockSpec((gb, rb, d), lambda i, j: (i, j, 0)),
                  pl.BlockSpec((gb, 1, d), lambda i, j: (i, 0, 0)),
                  pl.BlockSpec((rows, DN_WIDTH), lambda i, j: (i * nj + j, 0)),
                  pl.BlockSpec((rows, SB_WIDTH), lambda i, j: (i * nj + j, 0)),
                  pl.BlockSpec(w_out.shape, lambda i, j: (0, 0))],
        out_specs=pl.BlockSpec((gb, rb, d), lambda i, j: (i, j, 0)),
        out_shape=jax.ShapeDtypeStruct(x.shape, F32),
        compiler_params=_params(("arbitrary", "arbitrary")),
        name="proj_out",
    )(x, gate, o_dn, o_sb, w_out)


def _dn_masks(group):
    c = DN_CHUNK
    i = jnp.arange(c)[:, None]
    m = jnp.arange(c)[None, :]
    same = (i // group) == (m // group)
    lower = jnp.logical_and(m <= i, same)
    lt = jnp.concatenate([lower, same], axis=0).astype(BF16)
    sext = jnp.concatenate([(i > m), jnp.ones((c, c), bool)], axis=1).astype(F32)
    return lt, sext


def _dn_gates(ab, alog, dtb):
    g = -jnp.exp(alog) * _softplus(ab + dtb)
    beta = _sigmoid(ab)
    return g, beta


def _dn_local(items, lt, sext, group):
    c = DN_CHUNK
    row = lax.broadcasted_iota(jnp.int32, (c, c), 0)
    col = lax.broadcasted_iota(jnp.int32, (c, c), 1)
    lower = col <= row
    strict = col < row
    if group < c:
        shift = group.bit_length() - 1
        same = (row >> shift) == (col >> shift)
        lower = jnp.logical_and(lower, same)
        strict = jnp.logical_and(strict, same)
    eye = (row == col).astype(F32)
    n_sq = max(1, (min(group, c) - 1).bit_length() - 1)

    qs, ks, kbs, dms, gams, tots = [], [], [], [], [], []
    for q_raw, k_raw, _, g_col, beta_col in items:
        qs.append(q_raw * lax.rsqrt(jnp.sum(q_raw * q_raw, axis=-1, keepdims=True) + L2_EPS)
                  * (DN_HEAD_DIM ** -0.5))
        k = k_raw * lax.rsqrt(jnp.sum(k_raw * k_raw, axis=-1, keepdims=True) + L2_EPS)
        ks.append(k)
        kbs.append(k * beta_col)
    for _, _, _, g_col, _ in items:
        r = _mm_mask_left(lt, g_col * sext)
        diff = r[0:c, 0:c]
        gams.append(r[0:c, c:c + 1])
        tots.append(r[c:2 * c, c:c + 1])
        dms.append(jnp.where(lower, jnp.exp(jnp.where(lower, diff, 0.0)), 0.0))
    a_s = [jnp.where(strict, _mm3(kb, k, NT) * dm, 0.0) for kb, k, dm in zip(kbs, ks, dms)]
    attns = [_mm3(q, k, NT) * dm for q, k, dm in zip(qs, ks, dms)]
    ps = [eye - a for a in a_s]
    xs = a_s
    for _ in range(n_sq):
        xs = [_mm3(x, x) for x in xs]
        ps = [p + _mm3(p, x) for p, x in zip(ps, xs)]
    egs = [jnp.exp(gam) for gam in gams]
    uws = [_mm3(p, jnp.concatenate([it[2] * it[4], kb * eg], axis=1))
           for p, it, kb, eg in zip(ps, items, kbs, egs)]
    out = []
    for q, k, uw, attn, eg, gam, tot in zip(qs, ks, uws, attns, egs, gams, tots):
        out.append((uw[:, 0:DN_HEAD_DIM], uw[:, DN_HEAD_DIM:2 * DN_HEAD_DIM], attn,
                    q * eg, k * jnp.exp(tot - gam), tot))
    return out


def _dn_items(qkv, g_all, beta_all, r0):
    c = DN_CHUNK
    dh = DN_HEAD_DIM
    items = []
    for h in range(DN_HEADS):
        items.append((qkv[r0:r0 + c, h * dh:(h + 1) * dh],
                      qkv[r0:r0 + c, DN_WIDTH + h * dh:DN_WIDTH + (h + 1) * dh],
                      qkv[r0:r0 + c, 2 * DN_WIDTH + h * dh:2 * DN_WIDTH + (h + 1) * dh],
                      g_all[r0:r0 + c, h:h + 1],
                      beta_all[r0:r0 + c, DN_HEADS + h:DN_HEADS + h + 1]))
    return items


def _dn_out(o, z, ng):
    y = o * lax.rsqrt(jnp.mean(o * o, axis=-1, keepdims=True) + NORM_EPS) * ng
    return y * _silu(z)


def _dn_prompt_kernel(qkv_ref, z_ref, ab_ref, cw_ref, alog_ref, dtb_ref, ng_ref, lt_ref, sext_ref,
                      o_ref, s_out_ref, conv_out_ref, xp_scr, s_scr):
    c = DN_CHUNK
    dh = DN_HEAD_DIM
    tb = qkv_ref.shape[0]
    step = pl.program_id(0)

    @pl.when(step == 0)
    def _():
        xp_scr[0:8, :] = jnp.zeros((8, DN_CONV_DIM), F32)
        s_scr[...] = jnp.zeros_like(s_scr)

    x = qkv_ref[...]
    xp_scr[8:8 + tb, :] = x
    cw = cw_ref[...]
    y = xp_scr[5:5 + tb, :] * cw[0:1, :]
    y = y + xp_scr[6:6 + tb, :] * cw[1:2, :]
    y = y + xp_scr[7:7 + tb, :] * cw[2:3, :]
    y = y + x * cw[3:4, :]
    conv_out_ref[...] = xp_scr[5 + tb:8 + tb, :]
    xp_scr[0:8, :] = xp_scr[tb:tb + 8, :]
    qkv = _silu(y)

    g_all, beta_all = _dn_gates(ab_ref[...], alog_ref[...], dtb_ref[...])
    ng = ng_ref[...]
    items = []
    for n in range(tb // c):
        items += _dn_items(qkv, g_all, beta_all, n * c)
    local = _dn_local(items, lt_ref[...], sext_ref[...], c)
    s = [s_scr[h] for h in range(DN_HEADS)]
    for n in range(tb // c):
        loc = local[n * DN_HEADS:(n + 1) * DN_HEADS]
        v_new = [u - _mm3(w, s[h]) for h, (u, w, _, _, _, _) in enumerate(loc)]
        o = [_mm3(q_dec, s[h]) + _mm3(attn, v_new[h]) for h, (_, _, attn, q_dec, _, _) in enumerate(loc)]
        s = [s[h] * jnp.exp(tot[0:1, :]) + _mm3(k_dec, v_new[h], TN)
             for h, (_, _, _, _, k_dec, tot) in enumerate(loc)]
        for h in range(DN_HEADS):
            o_ref[n * c:(n + 1) * c, h * dh:(h + 1) * dh] = _dn_out(
                o[h], z_ref[n * c:(n + 1) * c, h * dh:(h + 1) * dh], ng)
    for h in range(DN_HEADS):
        s_scr[h] = s[h]
        s_out_ref[h] = s[h]


def _dn_prompt(qkv, z, ab, conv_w, alog, dtb, ng, chunks_per_step=4):
    t = qkv.shape[0]
    c = DN_CHUNK * min(chunks_per_step, t // DN_CHUNK)
    assert t % c == 0
    lt, sext = _dn_masks(DN_CHUNK)
    const = lambda shape: pl.BlockSpec(shape, lambda i: tuple(0 for _ in shape))
    return pl.pallas_call(
        _dn_prompt_kernel,
        grid=(t // c,),
        in_specs=[pl.BlockSpec((c, DN_CONV_DIM), lambda i: (i, 0)),
                  pl.BlockSpec((c, DN_WIDTH), lambda i: (i, 0)),
                  pl.BlockSpec((c, LANES), lambda i: (i, 0)),
                  const((DN_CONV_W, DN_CONV_DIM)), const((1, LANES)), const((1, LANES)),
                  const((1, DN_HEAD_DIM)), const(lt.shape), const(sext.shape)],
        out_specs=[pl.BlockSpec((c, DN_WIDTH), lambda i: (i, 0)),
                   const((DN_HEADS, DN_HEAD_DIM, DN_HEAD_DIM)),
                   const((DN_CONV_W - 1, DN_CONV_DIM))],
        out_shape=[jax.ShapeDtypeStruct((t, DN_WIDTH), F32),
                   jax.ShapeDtypeStruct((DN_HEADS, DN_HEAD_DIM, DN_HEAD_DIM), F32),
                   jax.ShapeDtypeStruct((DN_CONV_W - 1, DN_CONV_DIM), F32)],
        scratch_shapes=[pltpu.VMEM((c + 8, DN_CONV_DIM), F32),
                        pltpu.VMEM((DN_HEADS, DN_HEAD_DIM, DN_HEAD_DIM), F32)],
        compiler_params=_params(("arbitrary",)),
        name="deltanet_prompt",
    )(qkv, z, ab, conv_w, alog, dtb, ng, lt, sext)


def _dn_sample_kernel(qkv_ref, z_ref, ab_ref, cprev_ref, s_ref, cw_ref, alog_ref, dtb_ref, ng_ref,
                      lt_ref, sext_ref, o_ref, s_out_ref, conv_out_ref, xp_scr, *, t):
    c = DN_CHUNK
    dh = DN_HEAD_DIM
    nreq = c // t
    x3 = qkv_ref[...]
    xp_scr[:, 8 - (DN_CONV_W - 1):8, :] = cprev_ref[...]
    xp_scr[:, 8:8 + t, :] = x3
    cw = cw_ref[...]
    y = xp_scr[:, 5:5 + t, :] * cw[0:1, :]
    y = y + xp_scr[:, 6:6 + t, :] * cw[1:2, :]
    y = y + xp_scr[:, 7:7 + t, :] * cw[2:3, :]
    y = y + x3 * cw[3:4, :]
    conv_out_ref[...] = xp_scr[:, 5 + t:8 + t, :]
    qkv = _silu(y).reshape(c, DN_CONV_DIM)

    g_all, beta_all = _dn_gates(ab_ref[...], alog_ref[...], dtb_ref[...])
    ng = ng_ref[...]
    rowid = lax.broadcasted_iota(jnp.int32, (c, 1), 0)
    local = _dn_local(_dn_items(qkv, g_all, beta_all, 0), lt_ref[...], sext_ref[...], t)
    for h in range(DN_HEADS):
        u, w, attn, q_dec, k_dec, tot = local[h]
        ws, qs = [], []
        for r in range(nreq):
            s = s_ref[r, h]
            ws.append(_mm3(w, s)[r * t:(r + 1) * t, :])
            qs.append(_mm3(q_dec, s)[r * t:(r + 1) * t, :])
        v_new = u - jnp.concatenate(ws, axis=0)
        o = jnp.concatenate(qs, axis=0) + _mm3(attn, v_new)
        for r in range(nreq):
            in_req = jnp.logical_and(rowid >= r * t, rowid < (r + 1) * t)
            kd_r = jnp.where(in_req, k_dec, 0.0)
            s_out_ref[r, h] = (s_ref[r, h] * jnp.exp(tot[r * t:r * t + 1, :])
                               + _mm3(kd_r, v_new, TN))
        o_ref[:, h * dh:(h + 1) * dh] = _dn_out(o, z_ref[:, h * dh:(h + 1) * dh], ng)


def _dn_sample(qkv, z, ab, conv_prev, ssm_prev, conv_w, alog, dtb, ng, t):
    n = qkv.shape[0]
    b = n // t
    c = DN_CHUNK
    assert t == 8 and c % t == 0 and b % (c // t) == 0
    nreq = c // t
    lt, sext = _dn_masks(t)
    const = lambda shape: pl.BlockSpec(shape, lambda i: tuple(0 for _ in shape))
    return pl.pallas_call(
        functools.partial(_dn_sample_kernel, t=t),
        grid=(b // nreq,),
        in_specs=[pl.BlockSpec((nreq, t, DN_CONV_DIM), lambda i: (i, 0, 0)),
                  pl.BlockSpec((c, DN_WIDTH), lambda i: (i, 0)),
                  pl.BlockSpec((c, LANES), lambda i: (i, 0)),
                  pl.BlockSpec((nreq, DN_CONV_W - 1, DN_CONV_DIM), lambda i: (i, 0, 0)),
                  pl.BlockSpec((nreq, DN_HEADS, DN_HEAD_DIM, DN_HEAD_DIM), lambda i: (i, 0, 0, 0)),
                  const((DN_CONV_W, DN_CONV_DIM)), const((1, LANES)), const((1, LANES)),
                  const((1, DN_HEAD_DIM)), const(lt.shape), const(sext.shape)],
        out_specs=[pl.BlockSpec((c, DN_WIDTH), lambda i: (i, 0)),
                   pl.BlockSpec((nreq, DN_HEADS, DN_HEAD_DIM, DN_HEAD_DIM), lambda i: (i, 0, 0, 0)),
                   pl.BlockSpec((nreq, DN_CONV_W - 1, DN_CONV_DIM), lambda i: (i, 0, 0))],
        out_shape=[jax.ShapeDtypeStruct((n, DN_WIDTH), F32),
                   jax.ShapeDtypeStruct(ssm_prev.shape, F32),
                   jax.ShapeDtypeStruct(conv_prev.shape, F32)],
        scratch_shapes=[pltpu.VMEM((nreq, 8 + t, DN_CONV_DIM), F32)],
        compiler_params=_params(("arbitrary",)),
        name="deltanet_sample",
    )(qkv.reshape(b, t, DN_CONV_DIM), z, ab, conv_prev, ssm_prev, conv_w, alog, dtb, ng, lt, sext)


LOG2E = 1.4426950408889634


def _sb_softplus(z2, vis):
    sp = jnp.maximum(z2, 0.0) + jnp.log2(1.0 + jnp.exp2(-jnp.abs(z2)))
    if vis is not None:
        sp = jnp.where(vis, sp, 0.0)
    return sp.astype(BF16)


def _sb_weights(z2, cum, carry, vis):
    w = jnp.exp2(z2 + cum + carry)
    if vis is not None:
        w = jnp.where(vis, w, 0.0)
    return w.astype(BF16)


def _neg_upper(tk):
    j = jnp.arange(tk)[:, None]
    s = jnp.arange(tk)[None, :]
    return -(j >= s).astype(BF16)


def _div(x, d):
    if d & (d - 1) == 0:
        return x >> (d.bit_length() - 1)
    return lax.div(x, jnp.int32(d))


SB_CHUNK_PAGES = 8


def _sb_kernel(pt_ref, bias_ref, q_ref, k_ref, v_ref, u_ref, g_ref,
               qs_ref, kn_ref, vn_ref, brow_ref, us_ref, gs_ref, poolk_ref, poolv_ref,
               o_ref, os_ref,
               acc_scr, carry_scr, z0_scr, z1_scr, w0_scr, w1_scr,
               kbuf, vbuf, sacc_scr, scarry_scr, cnt_scr, sem, *, tq, tk, n_pages):
    z_scr = (z0_scr, z1_scr)
    w_scr = (w0_scr, w1_scr)
    p = pl.program_id(0)
    i = pl.program_id(1)
    first_step = jnp.logical_and(p == 0, i == 0)
    last_step = jnp.logical_and(p == pl.num_programs(0) - 1, i == pl.num_programs(1) - 1)
    nsub = tq // tk
    chains = [(hh, r) for hh in range(2) for r in range(nsub)]
    nch = len(chains)
    n_tiles = nsub * (i + 1)
    lane = lax.broadcasted_iota(jnp.int32, (1, LANES), 1)
    head_lanes = [lane < SB_HEAD_DIM, lane >= SB_HEAD_DIM]
    qs = q_ref[...] * (SB_HEAD_DIM ** -0.5 * LOG2E)
    qh = {(hh, r): jnp.where(head_lanes[hh], qs[r * tk:(r + 1) * tk, :], 0.0).astype(BF16)
          for hh, r in chains}
    bias = [bias_ref[2 * p] * LOG2E, bias_ref[2 * p + 1] * LOG2E]
    uneg = u_ref[...]
    col_minus_row = (lax.broadcasted_iota(jnp.int32, (tk, tk), 1)
                     - lax.broadcasted_iota(jnp.int32, (tk, tk), 0))
    acc_scr[...] = jnp.zeros_like(acc_scr)
    carry_scr[...] = jnp.zeros_like(carry_scr)

    n_req, t, _ = qs_ref.shape
    page = kbuf.shape[-1]
    npg = SB_CHUNK_PAGES
    parts = n_pages // npg
    n_chunks = n_req * parts
    rows = SB_HEADS * t
    t_shift = t.bit_length() - 1
    head_shift = SB_HEAD_DIM.bit_length() - 1

    def page_copies(c):
        slot = c & 1
        r = _div(c, parts)
        part = c - r * parts
        copies = []
        for n in range(npg):
            pg = pt_ref[r * n_pages + (n_pages - 1 - (part * npg + n))]
            copies.append(pltpu.make_async_copy(poolk_ref.at[pg], kbuf.at[slot, n], sem.at[0, slot]))
            copies.append(pltpu.make_async_copy(poolv_ref.at[pg], vbuf.at[slot, n], sem.at[1, slot]))
        return copies

    def fetch_next_and_wait(c):
        @pl.when(c + 1 < n_chunks)
        def _():
            for cp in page_copies(c + 1):
                cp.start()

        for cp in page_copies(c):
            cp.wait()

    def sample_chunk(c):
        slot = c & 1
        r = _div(c, parts)
        first = (c - r * parts) == 0
        st = {}

        def logits():
            brow = brow_ref[...] * LOG2E
            row = lax.broadcasted_iota(jnp.int32, (rows, SB_WIDTH), 0)
            col = lax.broadcasted_iota(jnp.int32, (rows, SB_WIDTH), 1)
            q8 = qs_ref[r] * (SB_HEAD_DIM ** -0.5 * LOG2E)
            qt = jnp.concatenate([q8] * SB_HEADS, axis=0)
            qbd = jnp.where((row >> t_shift) == (col >> head_shift), qt, 0.0).astype(BF16)
            pad = jnp.zeros((page - t, SB_WIDTH), F32)
            kn = jnp.concatenate([kn_ref[r], pad], axis=0).astype(BF16)
            st["vn"] = jnp.concatenate([vn_ref[r], pad], axis=0).astype(BF16)
            zs = [_dot(qbd, kn, NT) + brow]
            for n in range(npg):
                zs.append(_dot(qbd, kbuf[slot, n].astype(BF16)) + brow)
            st["zs"] = zs

        def sums():
            krow = lax.broadcasted_iota(jnp.int32, (rows, page), 0)
            kcol = lax.broadcasted_iota(jnp.int32, (rows, page), 1)
            vis = kcol < jnp.where(first, krow & (t - 1), 0)
            st["vis"] = vis
            us = us_ref[...]
            zs = st["zs"]
            st["cums"] = ([_dot(_sb_softplus(zs[0], vis), us)]
                          + [_dot(_sb_softplus(z, None), us) for z in zs[1:]])

        def values():
            zs, cums = st["zs"], st["cums"]
            carry = jnp.where(first, 0.0, scarry_scr[...])
            acc = jnp.where(first, 0.0, sacc_scr[...])
            acc = acc + _dot(_sb_weights(zs[0], cums[0], carry, st["vis"]), st["vn"])
            carry = carry + cums[0][:, 0:1]
            for n in range(npg):
                w = _sb_weights(zs[n + 1], cums[n + 1], carry, None)
                acc = acc + _dot(w, vbuf[slot, n].astype(BF16), NT)
                carry = carry + cums[n + 1][:, 0:1]
            sacc_scr[...] = acc
            scarry_scr[...] = carry

        return logits, sums, values

    def sample_output(c):
        acc = sacc_scr[...]
        col = lax.broadcasted_iota(jnp.int32, (t, SB_WIDTH), 1)
        out = jnp.zeros((t, SB_WIDTH), F32)
        for h in range(SB_HEADS):
            out = out + jnp.where((col >> head_shift) == h, acc[h * t:(h + 1) * t, :], 0.0)
        sq = out * out
        ms = jnp.zeros((t, SB_WIDTH), F32)
        for h in range(SB_HEADS):
            in_h = (col >> head_shift) == h
            ms_h = jnp.sum(jnp.where(in_h, sq, 0.0), axis=-1, keepdims=True) * (1.0 / SB_HEAD_DIM)
            ms = jnp.where(in_h, ms_h, ms)
        os_ref[_div(c, parts)] = out * lax.rsqrt(ms + NORM_EPS) * gs_ref[...]

    @pl.when(first_step)
    def _():
        cnt_scr[0] = 0
        sacc_scr[...] = jnp.zeros_like(sacc_scr)
        scarry_scr[...] = jnp.zeros_like(scarry_scr)
        for cp in page_copies(0):
            cp.start()

    def key_tile(n):
        j = jnp.maximum(n_tiles - 1 - n, 0)
        return j, pl.multiple_of(j * tk, tk)

    def logits(n, slot, ci):
        _, start = key_tile(n)
        kt = k_ref[:, pl.ds(start, tk)]
        z_scr[slot][ci] = _dot(qh[chains[ci]], kt) + bias[chains[ci][0]]

    def values(n, slot, ci):
        _, start = key_tile(n)
        vt = v_ref[:, pl.ds(start, tk)]
        acc_scr[ci] += _dot(w_scr[slot][ci], vt, NT)

    def visit_phases(n, slot, masked, with_values):
        j, _ = key_tile(n)
        if masked:
            vis = [col_minus_row < (i * tq + r * tk - j * tk) for _, r in chains]
        else:
            vis = [None] * nch
        zs, sps, cums = [], [], []

        def phase1():
            for ci in range(nch):
                zs.append(z_scr[slot][ci])
                sps.append(_sb_softplus(zs[ci], vis[ci]))
                logits(n + 1, 1 - slot, ci)

        def phase2():
            for ci in range(nch):
                cums.append(_dot(sps[ci], uneg))
                if with_values:
                    values(n - 1, 1 - slot, ci)

        def phase3():
            for ci in range(nch):
                w_scr[slot][ci] = _sb_weights(zs[ci], cums[ci], carry_scr[ci], vis[ci])
                carry_scr[ci] += cums[ci][:, 0:1]

        return phase1, phase2, phase3

    def visit(n, slot, masked, with_values):
        for phase in visit_phases(n, slot, masked, with_values):
            phase()

    for ci in range(nch):
        logits(0, 0, ci)
    for n in range(nsub):
        visit(n, n % 2, True, n > 0)

    n_trips = (n_tiles - nsub) // 2
    c0 = cnt_scr[0]
    n_fused = jnp.minimum(n_trips, n_chunks - c0)

    def fused_trip(m, carry):
        c = c0 + m
        fetch_next_and_wait(c)
        s_logits, s_sums, s_values = sample_chunk(c)
        n = nsub + 2 * m
        a1, a2, a3 = visit_phases(n, nsub % 2, False, True)
        b1, b2, b3 = visit_phases(n + 1, (nsub + 1) % 2, False, True)
        sample_output(jnp.maximum(c - 1, 0))
        s_logits()
        a1()
        a2()
        s_sums()
        a3()
        b1()
        b2()
        s_values()
        b3()
        return carry

    def plain_trip(m, carry):
        n = nsub + 2 * m
        visit(n, nsub % 2, False, True)
        visit(n + 1, (nsub + 1) % 2, False, True)
        return carry

    lax.fori_loop(0, n_fused, fused_trip, 0)
    lax.fori_loop(n_fused, n_trips, plain_trip, 0)
    cnt_scr[0] = c0 + n_fused
    for ci in range(nch):
        values(n_tiles - 1, (nsub - 1) % 2, ci)
    for r in range(nsub):
        out = jnp.where(head_lanes[0], acc_scr[chains.index((0, r))], acc_scr[chains.index((1, r))])
        sq = out * out
        ms = [jnp.sum(jnp.where(m, sq, 0.0), axis=-1, keepdims=True) * (1.0 / SB_HEAD_DIM)
              for m in head_lanes]
        ms = jnp.where(head_lanes[0], ms[0], ms[1])
        o_ref[r * tk:(r + 1) * tk, :] = out * lax.rsqrt(ms + NORM_EPS) * g_ref[...]

    @pl.when(last_step)
    def _():
        def rest(c, carry):
            fetch_next_and_wait(c)
            sample_output(jnp.maximum(c - 1, 0))
            for stage in sample_chunk(c):
                stage()
            return carry

        lax.fori_loop(c0 + n_fused, n_chunks, rest, 0)
        sample_output(n_chunks - 1)


def _sb_attention(q, kt_bf, vt_bf, q_s, k_new, v_new, pool_k, pool_v, page_table, bias, g_sb,
                  tq=512, tk=256):
    t_len = q.shape[0]
    tq = min(tq, t_len)
    assert t_len % tq == 0 and tq % (2 * tk) == 0
    b, t, _ = q_s.shape
    n_pages = page_table.shape[1]
    page = pool_k.shape[2]
    npg = SB_CHUNK_PAGES
    assert n_pages % npg == 0 and page == LANES and t == 8
    npairs = SB_WIDTH // LANES
    nchains = 2 * (tq // tk)
    rows = SB_HEADS * t
    brow = jnp.broadcast_to(jnp.repeat(bias, t)[:, None], (rows, LANES))
    g2 = jnp.tile(g_sb, LANES // SB_HEAD_DIM).reshape(1, LANES)
    g8 = jnp.tile(g_sb, SB_HEADS).reshape(1, SB_WIDTH)
    smem = pl.BlockSpec(memory_space=pltpu.SMEM)
    hbm = pl.BlockSpec(memory_space=pl.ANY)
    whole = lambda a: pl.BlockSpec(a.shape, lambda p, i: (0,) * a.ndim, pipeline_mode=pl.Buffered(1))
    return pl.pallas_call(
        functools.partial(_sb_kernel, tq=tq, tk=tk, n_pages=n_pages),
        grid=(npairs, t_len // tq),
        in_specs=[smem, smem,
                  pl.BlockSpec((tq, LANES), lambda p, i: (i, p)),
                  pl.BlockSpec((LANES, t_len), lambda p, i: (p, 0)),
                  pl.BlockSpec((LANES, t_len), lambda p, i: (p, 0)),
                  pl.BlockSpec((tk, tk), lambda p, i: (0, 0)),
                  pl.BlockSpec((1, LANES), lambda p, i: (0, 0)),
                  whole(q_s), whole(k_new), whole(v_new),
                  pl.BlockSpec((rows, LANES), lambda p, i: (0, 0)),
                  pl.BlockSpec((page, page), lambda p, i: (0, 0)),
                  pl.BlockSpec((1, SB_WIDTH), lambda p, i: (0, 0)),
                  hbm, hbm],
        out_specs=[pl.BlockSpec((tq, LANES), lambda p, i: (i, p)),
                   pl.BlockSpec((b, t, SB_WIDTH), lambda p, i: (0, 0, 0))],
        out_shape=[jax.ShapeDtypeStruct((t_len, SB_WIDTH), F32),
                   jax.ShapeDtypeStruct((b, t, SB_WIDTH), F32)],
        scratch_shapes=[pltpu.VMEM((nchains, tk, LANES), F32),
                        pltpu.VMEM((nchains, tk, 1), F32),
                        pltpu.VMEM((nchains, tk, tk), F32), pltpu.VMEM((nchains, tk, tk), F32),
                        pltpu.VMEM((nchains, tk, tk), BF16), pltpu.VMEM((nchains, tk, tk), BF16),
                        pltpu.VMEM((2, npg, SB_WIDTH, page), F32),
                        pltpu.VMEM((2, npg, SB_WIDTH, page), F32),
                        pltpu.VMEM((rows, SB_WIDTH), F32),
                        pltpu.VMEM((rows, 1), F32),
                        pltpu.SMEM((1,), jnp.int32),
                        pltpu.SemaphoreType.DMA((2, 2))],
        compiler_params=_params(("arbitrary", "arbitrary")),
        name="sb_attention",
    )(page_table.reshape(-1), bias, q, kt_bf, vt_bf, _neg_upper(tk), g2,
      q_s, k_new, v_new, brow, _neg_upper(page), g8, pool_k, pool_v)


def _split_w_in(w_in):
    d = w_in.shape[0]
    o_ab = DN_CONV_DIM + DN_WIDTH
    o_q = o_ab + 2 * DN_HEADS
    o_k = o_q + SB_WIDTH
    o_v = o_k + SB_WIDTH
    pad = jnp.zeros((d, LANES - 2 * DN_HEADS), w_in.dtype)
    w_r = jnp.concatenate([w_in[:, 0:o_ab], w_in[:, o_q:o_k], w_in[:, o_ab:o_q], pad], axis=1)
    return w_r.astype(BF16), w_in[:, o_k:o_v].astype(BF16), w_in[:, o_v:].astype(BF16)


def _heads_last(xt, n):
    return jnp.transpose(xt.reshape(SB_HEADS, SB_HEAD_DIM, n), (2, 0, 1))


def _lane_row(v):
    return jnp.pad(v.astype(F32), (0, LANES - v.shape[0])).reshape(1, LANES)


def kernel(x_prompt, x_sample, c_prompt, c_sample, cache_sb_k, cache_sb_v, page_table, state_dn_ssm, state_dn_conv, w_ada, b_ada, g_ffn1, w1_gate, w1_up, w1_down, g_mix, w_in, dn_conv_w, dn_a_log, dn_dt_bias, dn_norm_g, sb_norm_g, sb_bias, w_out, g_ffn2, w2_gate, w2_up, w2_down, g_final):
    depth = w_ada.shape[0]
    bsz, seq, d = x_prompt.shape
    dec_b, dec_t, _ = x_sample.shape
    assert bsz == 1
    xp, xs = x_prompt, x_sample
    c_all = jnp.concatenate([c_prompt, c_sample], axis=0)
    n_c = c_all.shape[0]
    c_all = jnp.pad(c_all, ((0, (-n_c) % 8), (0, 0)))
    outs = {name: [] for name in ("kp", "vp", "ssmp", "convp", "ks", "vs", "ssms", "convs")}
    for l in range(depth):
        last = l == depth - 1
        mods = _adaln(c_all, w_ada[l].astype(BF16), b_ada[l]).reshape(-1, N_MOD, d)
        mp = [mods[0:bsz, i:i + 1, :] for i in range(N_MOD)]
        ms = [mods[bsz:bsz + dec_b, i:i + 1, :] for i in range(N_MOD)]
        w1 = (w1_gate[l].astype(BF16), w1_up[l].astype(BF16), w1_down[l].astype(BF16))
        w2 = (w2_gate[l].astype(BF16), w2_up[l].astype(BF16), w2_down[l].astype(BF16))
        w_in_r, w_k, w_v = _split_w_in(w_in[l])
        w_out_b = w_out[l].astype(BF16)
        alog = _lane_row(dn_a_log[l])
        dtb = _lane_row(dn_dt_bias[l])
        ng = dn_norm_g[l].reshape(1, DN_HEAD_DIM).astype(F32)
        g_sb = sb_norm_g[l].astype(F32)
        bias = sb_bias[l].astype(F32)

        xp = _ffn(xp, mp[0], mp[1], mp[2], g_ffn1[l], *w1)
        qkv, z, q_p, ab, kt, vt, kt_bf, vt_bf = _proj(xp, mp[3], mp[4], g_mix[l], w_in_r, w_k.T, w_v.T, True)
        o_dn_p, ssm_p, conv_p = _dn_prompt(qkv, z, ab, dn_conv_w[l], alog, dtb, ng)
        outs["kp"].append(_heads_last(kt, seq).reshape(bsz, seq, SB_HEADS, SB_HEAD_DIM))
        outs["vp"].append(_heads_last(vt, seq).reshape(bsz, seq, SB_HEADS, SB_HEAD_DIM))
        outs["ssmp"].append(ssm_p.reshape(bsz, DN_HEADS, DN_HEAD_DIM, DN_HEAD_DIM))
        outs["convp"].append(conv_p.reshape(bsz, DN_CONV_W - 1, DN_CONV_DIM))

        xs = _ffn(xs, ms[0], ms[1], ms[2], g_ffn1[l], *w1)
        qkv, z, q, ab, k, v = _proj(xs, ms[3], ms[4], g_mix[l], w_in_r, w_k, w_v, False)
        o_dn, ssm_s, conv_s = _dn_sample(qkv, z, ab, state_dn_conv[l], state_dn_ssm[l],
                                         dn_conv_w[l], alog, dtb, ng, dec_t)
        n_pool, page = cache_sb_k.shape[1], cache_sb_k.shape[2]
        pool_k = jnp.transpose(cache_sb_k[l], (0, 2, 3, 1)).reshape(n_pool, SB_WIDTH, page)
        pool_v = jnp.transpose(cache_sb_v[l], (0, 2, 3, 1)).reshape(n_pool, SB_WIDTH, page)
        o_sb_p, o_sb_s = _sb_attention(q_p, kt_bf, vt_bf, q.reshape(dec_b, dec_t, SB_WIDTH),
                                       k.reshape(dec_b, dec_t, SB_WIDTH), v.reshape(dec_b, dec_t, SB_WIDTH),
                                       pool_k, pool_v, page_table, bias, g_sb)
        xp = _outproj(xp, mp[5], o_dn_p, o_sb_p, w_out_b)
        xp = _ffn(xp, mp[6], mp[7], mp[8], g_ffn2[l], *w2, gfin=g_final if last else None)
        xs = _outproj(xs, ms[5], o_dn, o_sb_s.reshape(dec_b * dec_t, SB_WIDTH), w_out_b)
        xs = _ffn(xs, ms[6], ms[7], ms[8], g_ffn2[l], *w2, gfin=g_final if last else None)
        outs["ks"].append(k.reshape(dec_b, dec_t, SB_HEADS, SB_HEAD_DIM))
        outs["vs"].append(v.reshape(dec_b, dec_t, SB_HEADS, SB_HEAD_DIM))
        outs["ssms"].append(ssm_s)
        outs["convs"].append(conv_s)
    if depth == 0:
        raise ValueError("depth must be positive")
    return (xp, xs,
            jnp.stack(outs["kp"]), jnp.stack(outs["vp"]), jnp.stack(outs["ssmp"]), jnp.stack(outs["convp"]),
            jnp.stack(outs["ks"]), jnp.stack(outs["vs"]), jnp.stack(outs["ssms"]), jnp.stack(outs["convs"]))
```

```python
import functools

import jax
import jax.numpy as jnp
from jax import lax
from jax.experimental import pallas as pl
from jax.experimental.pallas import tpu as pltpu

F32 = jnp.float32
BF16 = jnp.bfloat16

DN_HEADS = 4
DN_HEAD_DIM = 128
DN_WIDTH = DN_HEADS * DN_HEAD_DIM
DN_CONV_W = 4
DN_CONV_DIM = 3 * DN_WIDTH
DN_CHUNK = 64
SB_HEADS = 8
SB_HEAD_DIM = 64
SB_WIDTH = SB_HEADS * SB_HEAD_DIM
N_MOD = 9
NORM_EPS = 1e-6
L2_EPS = 1e-6
LANES = 128

NN = (((1,), (0,)), ((), ()))
NT = (((1,), (1,)), ((), ()))
TN = (((0,), (0,)), ((), ()))

VMEM_LIMIT = 56 * 1024 * 1024


def _params(sem):
    return pltpu.CompilerParams(dimension_semantics=sem, vmem_limit_bytes=VMEM_LIMIT)


def _dot(a, b, dims=NN):
    return lax.dot_general(a, b, dims, preferred_element_type=F32)


def _split2(x):
    hi = x.astype(BF16)
    lo = (x - hi.astype(F32)).astype(BF16)
    return hi, lo


def _split3(x):
    hi = x.astype(BF16)
    r = x - hi.astype(F32)
    mid = r.astype(BF16)
    lo = (r - mid.astype(F32)).astype(BF16)
    return hi, mid, lo


def _mm3(a, b, dims=NN):
    ah, al = _split2(a)
    bh, bl = _split2(b)
    return _dot(ah, bh, dims) + (_dot(ah, bl, dims) + _dot(al, bh, dims))


def _mm1(a, b, dims=NN):
    return _dot(a.astype(BF16), b.astype(BF16), dims)


def _mm_mask_left(m01, x):
    hi, mid, lo = _split3(x)
    return _dot(m01, hi) + (_dot(m01, mid) + _dot(m01, lo))


def _sigmoid(x):
    return 1.0 / (1.0 + jnp.exp(-x))


def _silu(x):
    return x * _sigmoid(x)


def _softplus(x):
    return jnp.maximum(x, 0.0) + jnp.log1p(jnp.exp(-jnp.abs(x)))


def _modnorm(x, gain, shift, scale):
    y = x * lax.rsqrt(jnp.mean(x * x, axis=-1, keepdims=True) + NORM_EPS)
    return (y * gain) * (1.0 + scale) + shift


def _adaln_kernel(c_ref, w_ref, b_ref, o_ref):
    c = _silu(c_ref[...]).astype(BF16)
    o_ref[...] = _dot(c, w_ref[...]) + b_ref[...]


def _adaln(c, w_bf, b):
    n, d = c.shape
    cols = w_bf.shape[1]
    tn = d
    return pl.pallas_call(
        _adaln_kernel,
        grid=(cols // tn,),
        in_specs=[pl.BlockSpec((n, d), lambda j: (0, 0)),
                  pl.BlockSpec((d, tn), lambda j: (0, j)),
                  pl.BlockSpec((1, tn), lambda j: (0, j))],
        out_specs=pl.BlockSpec((n, tn), lambda j: (0, j)),
        out_shape=jax.ShapeDtypeStruct((n, cols), F32),
        compiler_params=_params(("arbitrary",)),
        name="adaln",
    )(c, w_bf, b.reshape(1, cols))


def _row_blocks(g, r, target):
    if r >= target:
        assert r % target == 0
        return 1, target
    gb = min(g, target // r)
    assert g % gb == 0
    return gb, r


MXU_WIDTH = 256


def _ffn_kernel(x_ref, sh_ref, sc_ref, gt_ref, gain_ref, wg_ref, wu_ref, wd_ref, *rest, final_norm):
    if final_norm:
        gfin_ref, o_ref, acc_scr = rest
    else:
        o_ref, acc_scr = rest
    rows, d = acc_scr.shape
    x = x_ref[...]
    h = _modnorm(x, gain_ref[...], sh_ref[...], sc_ref[...]).reshape(rows, d).astype(BF16)
    dff = wg_ref.shape[1]
    for f in range(dff // MXU_WIDTH):
        cols = slice(f * MXU_WIDTH, (f + 1) * MXU_WIDTH)
        a = _dot(h, wg_ref[:, cols])
        b = _dot(h, wu_ref[:, cols])
        part = _dot((_silu(a) * b).astype(BF16), wd_ref[cols, :])
        if f == 0:
            acc_scr[...] = part
        else:
            acc_scr[...] += part
    y = x + 0.5 * gt_ref[...] * acc_scr[...].reshape(x.shape)
    if final_norm:
        y = y * lax.rsqrt(jnp.mean(y * y, axis=-1, keepdims=True) + NORM_EPS) * gfin_ref[...]
    o_ref[...] = y


def _ffn(x, shift, scale, gate, gain, wg, wu, wd, gfin=None, rows_target=512):
    g, r, d = x.shape
    dff = wg.shape[1]
    assert dff % MXU_WIDTH == 0
    gb, rb = _row_blocks(g, r, rows_target)
    rows = gb * rb
    mod_spec = pl.BlockSpec((gb, 1, d), lambda i, j: (i, 0, 0))
    vec_spec = pl.BlockSpec((1, d), lambda i, j: (0, 0))
    resident = lambda a: pl.BlockSpec(a.shape, lambda i, j: (0, 0), pipeline_mode=pl.Buffered(1))
    in_specs = [pl.BlockSpec((gb, rb, d), lambda i, j: (i, j, 0)),
                mod_spec, mod_spec, mod_spec, vec_spec,
                resident(wg), resident(wu), resident(wd)]
    args = [x, shift, scale, gate, gain.reshape(1, d), wg, wu, wd]
    if gfin is not None:
        in_specs.append(vec_spec)
        args.append(gfin.reshape(1, d))
    return pl.pallas_call(
        functools.partial(_ffn_kernel, final_norm=gfin is not None),
        grid=(g // gb, r // rb),
        in_specs=in_specs,
        out_specs=pl.BlockSpec((gb, rb, d), lambda i, j: (i, j, 0)),
        out_shape=jax.ShapeDtypeStruct(x.shape, F32),
        scratch_shapes=[pltpu.VMEM((rows, d), F32)],
        compiler_params=_params(("arbitrary", "arbitrary")),
        name="ffn",
    )(*args)


PROJ_COLS = (DN_CONV_DIM, DN_WIDTH, SB_WIDTH, LANES)


def _proj_kernel(x_ref, sh_ref, sc_ref, gain_ref, w_ref, wk_ref, wv_ref, *o_refs, transposed):
    rows = o_refs[0].shape[0]
    h = _modnorm(x_ref[...], gain_ref[...], sh_ref[...], sc_ref[...])
    h = h.reshape(rows, h.shape[-1]).astype(BF16)
    off = 0
    for o_ref, cols in zip(o_refs[0:len(PROJ_COLS)], PROJ_COLS):
        o_ref[...] = _dot(h, w_ref[:, off:off + cols])
        off += cols
    if transposed:
        k_ref, v_ref, kb_ref, vb_ref = o_refs[len(PROJ_COLS):]
        kt = _dot(wk_ref[...], h, NT)
        vt = _dot(wv_ref[...], h, NT)
        k_ref[...] = kt
        v_ref[...] = vt
        kb_ref[...] = kt.astype(BF16)
        vb_ref[...] = vt.astype(BF16)
    else:
        k_ref, v_ref = o_refs[len(PROJ_COLS):]
        k_ref[...] = _dot(h, wk_ref[...])
        v_ref[...] = _dot(h, wv_ref[...])


def _proj(x, shift, scale, gain, w_r, w_k, w_v, transposed, rows_target=512):
    g, r, d = x.shape
    gb, rb = _row_blocks(g, r, rows_target)
    rows = gb * rb
    nj = r // rb
    n = g * r
    mod_spec = pl.BlockSpec((gb, 1, d), lambda i, j: (i, 0, 0))
    const = lambda a: pl.BlockSpec(a.shape, lambda i, j: (0, 0))
    out_specs = [pl.BlockSpec((rows, c), lambda i, j: (i * nj + j, 0)) for c in PROJ_COLS]
    out_shape = [jax.ShapeDtypeStruct((n, c), F32) for c in PROJ_COLS]
    if transposed:
        out_specs += [pl.BlockSpec((SB_WIDTH, rows), lambda i, j: (0, i * nj + j))] * 4
        out_shape += [jax.ShapeDtypeStruct((SB_WIDTH, n), dt) for dt in (F32, F32, BF16, BF16)]
    else:
        out_specs += [pl.BlockSpec((rows, SB_WIDTH), lambda i, j: (i * nj + j, 0))] * 2
        out_shape += [jax.ShapeDtypeStruct((n, SB_WIDTH), F32)] * 2
    return pl.pallas_call(
        functools.partial(_proj_kernel, transposed=transposed),
        grid=(g // gb, nj),
        in_specs=[pl.BlockSpec((gb, rb, d), lambda i, j: (i, j, 0)),
                  mod_spec, mod_spec,
                  pl.BlockSpec((1, d), lambda i, j: (0, 0)),
                  const(w_r), const(w_k), const(w_v)],
        out_specs=out_specs,
        out_shape=out_shape,
        compiler_params=_params(("arbitrary", "arbitrary")),
        name="proj_in",
    )(x, shift, scale, gain.reshape(1, d), w_r, w_k, w_v)


def _outproj_kernel(x_ref, gt_ref, odn_ref, osb_ref, w_ref, o_ref):
    x = x_ref[...]
    mixed = (_dot(odn_ref[...].astype(BF16), w_ref[0:DN_WIDTH, :])
             + _dot(osb_ref[...].astype(BF16), w_ref[DN_WIDTH:DN_WIDTH + SB_WIDTH, :]))
    o_ref[...] = x + gt_ref[...] * mixed.reshape(x.shape)


def _outproj(x, gate, o_dn, o_sb, w_out, rows_target=512):
    g, r, d = x.shape
    gb, rb = _row_blocks(g, r, rows_target)
    rows = gb * rb
    nj = r // rb
    return pl.pallas_call(
        _outproj_kernel,
        grid=(g // gb, nj),
        in_specs=[pl.BlockSpec((gb, rb, d), lambda i, j: (i, j, 0)),
                  pl.BlockSpec((gb, 1, d), lambda i, j: (i, 0, 0)),
                  pl.BlockSpec((rows, DN_WIDTH), lambda i, j: (i * nj + j, 0)),
                  pl.BlockSpec((rows, SB_WIDTH), lambda i, j: (i * nj + j, 0)),
                  pl.BlockSpec(w_out.shape, lambda i, j: (0, 0))],
        out_specs=pl.BlockSpec((gb, rb, d), lambda i, j: (i, j, 0)),
        out_shape=jax.ShapeDtypeStruct(x.shape, F32),
        compiler_params=_params(("arbitrary", "arbitrary")),
        name="proj_out",
    )(x, gate, o_dn, o_sb, w_out)


def _dn_masks(group):
    c = DN_CHUNK
    i = jnp.arange(c)[:, None]
    m = jnp.arange(c)[None, :]
    same = (i // group) == (m // group)
    lower = jnp.logical_and(m <= i, same)
    lt = jnp.concatenate([lower, same], axis=0).astype(BF16)
    sext = jnp.concatenate([(i > m), jnp.ones((c, c), bool)], axis=1).astype(F32)
    return lt, sext


def _dn_gates(ab, alog, dtb):
    g = -jnp.exp(alog) * _softplus(ab + dtb)
    beta = _sigmoid(ab)
    return g, beta


def _dn_local(items, lt, sext, group):
    c = DN_CHUNK
    row = lax.broadcasted_iota(jnp.int32, (c, c), 0)
    col = lax.broadcasted_iota(jnp.int32, (c, c), 1)
    lower = col <= row
    strict = col < row
    if group < c:
        shift = group.bit_length() - 1
        same = (row >> shift) == (col >> shift)
        lower = jnp.logical_and(lower, same)
        strict = jnp.logical_and(strict, same)
    eye = (row == col).astype(F32)
    n_sq = max(1, (min(group, c) - 1).bit_length() - 1)

    qs, ks, kbs, dms, gams, tots = [], [], [], [], [], []
    for q_raw, k_raw, _, g_col, beta_col in items:
        qs.append(q_raw * lax.rsqrt(jnp.sum(q_raw * q_raw, axis=-1, keepdims=True) + L2_EPS)
                  * (DN_HEAD_DIM ** -0.5))
        k = k_raw * lax.rsqrt(jnp.sum(k_raw * k_raw, axis=-1, keepdims=True) + L2_EPS)
        ks.append(k)
        kbs.append(k * beta_col)
    for _, _, _, g_col, _ in items:
        r = _mm_mask_left(lt, g_col * sext)
        diff = r[0:c, 0:c]
        gams.append(r[0:c, c:c + 1])
        tots.append(r[c:2 * c, c:c + 1])
        dms.append(jnp.where(lower, jnp.exp(jnp.where(lower, diff, 0.0)), 0.0))
    a_s = [jnp.where(strict, _mm1(kb, k, NT) * dm, 0.0) for kb, k, dm in zip(kbs, ks, dms)]
    attns = [_mm1(q, k, NT) * dm for q, k, dm in zip(qs, ks, dms)]
    ps = [eye - a for a in a_s]
    xs = a_s
    for _ in range(n_sq):
        xs = [_mm3(x, x) for x in xs]
        ps = [p + _mm3(p, x) for p, x in zip(ps, xs)]
    egs = [jnp.exp(gam) for gam in gams]
    uws = [_mm1(p, jnp.concatenate([it[2] * it[4], kb * eg], axis=1))
           for p, it, kb, eg in zip(ps, items, kbs, egs)]
    out = []
    for q, k, uw, attn, eg, gam, tot in zip(qs, ks, uws, attns, egs, gams, tots):
        out.append((uw[:, 0:DN_HEAD_DIM], uw[:, DN_HEAD_DIM:2 * DN_HEAD_DIM], attn,
                    q * eg, k * jnp.exp(tot - gam), tot))
    return out


def _dn_items(qkv, g_all, beta_all, r0):
    c = DN_CHUNK
    dh = DN_HEAD_DIM
    items = []
    for h in range(DN_HEADS):
        items.append((qkv[r0:r0 + c, h * dh:(h + 1) * dh],
                      qkv[r0:r0 + c, DN_WIDTH + h * dh:DN_WIDTH + (h + 1) * dh],
                      qkv[r0:r0 + c, 2 * DN_WIDTH + h * dh:2 * DN_WIDTH + (h + 1) * dh],
                      g_all[r0:r0 + c, h:h + 1],
                      beta_all[r0:r0 + c, DN_HEADS + h:DN_HEADS + h + 1]))
    return items


def _dn_out(o, z, ng):
    y = o * lax.rsqrt(jnp.mean(o * o, axis=-1, keepdims=True) + NORM_EPS) * ng
    return y * _silu(z)


def _dn_prompt_kernel(qkv_ref, z_ref, ab_ref, cw_ref, alog_ref, dtb_ref, ng_ref, lt_ref, sext_ref,
                      o_ref, s_out_ref, conv_out_ref, xp_scr, s_scr):
    c = DN_CHUNK
    dh = DN_HEAD_DIM
    tb = qkv_ref.shape[0]
    step = pl.program_id(0)

    @pl.when(step == 0)
    def _():
        xp_scr[0:8, :] = jnp.zeros((8, DN_CONV_DIM), F32)
        s_scr[...] = jnp.zeros_like(s_scr)

    x = qkv_ref[...]
    xp_scr[8:8 + tb, :] = x
    cw = cw_ref[...]
    y = xp_scr[5:5 + tb, :] * cw[0:1, :]
    y = y + xp_scr[6:6 + tb, :] * cw[1:2, :]
    y = y + xp_scr[7:7 + tb, :] * cw[2:3, :]
    y = y + x * cw[3:4, :]
    conv_out_ref[...] = xp_scr[5 + tb:8 + tb, :]
    xp_scr[0:8, :] = xp_scr[tb:tb + 8, :]
    qkv = _silu(y)

    g_all, beta_all = _dn_gates(ab_ref[...], alog_ref[...], dtb_ref[...])
    ng = ng_ref[...]
    items = []
    for n in range(tb // c):
        items += _dn_items(qkv, g_all, beta_all, n * c)
    local = _dn_local(items, lt_ref[...], sext_ref[...], c)
    s = [s_scr[h] for h in range(DN_HEADS)]
    for n in range(tb // c):
        loc = local[n * DN_HEADS:(n + 1) * DN_HEADS]
        v_new = [u - _mm1(w, s[h]) for h, (u, w, _, _, _, _) in enumerate(loc)]
        o = [_mm1(q_dec, s[h]) + _mm1(attn, v_new[h]) for h, (_, _, attn, q_dec, _, _) in enumerate(loc)]
        s = [s[h] * jnp.exp(tot[0:1, :]) + _mm1(k_dec, v_new[h], TN)
             for h, (_, _, _, _, k_dec, tot) in enumerate(loc)]
        for h in range(DN_HEADS):
            o_ref[n * c:(n + 1) * c, h * dh:(h + 1) * dh] = _dn_out(
                o[h], z_ref[n * c:(n + 1) * c, h * dh:(h + 1) * dh], ng)
    for h in range(DN_HEADS):
        s_scr[h] = s[h]
        s_out_ref[h] = s[h]


def _dn_prompt(qkv, z, ab, conv_w, alog, dtb, ng, chunks_per_step=4):
    t = qkv.shape[0]
    c = DN_CHUNK * min(chunks_per_step, t // DN_CHUNK)
    assert t % c == 0
    lt, sext = _dn_masks(DN_CHUNK)
    const = lambda shape: pl.BlockSpec(shape, lambda i: tuple(0 for _ in shape))
    return pl.pallas_call(
        _dn_prompt_kernel,
        grid=(t // c,),
        in_specs=[pl.BlockSpec((c, DN_CONV_DIM), lambda i: (i, 0)),
                  pl.BlockSpec((c, DN_WIDTH), lambda i: (i, 0)),
                  pl.BlockSpec((c, LANES), lambda i: (i, 0)),
                  const((DN_CONV_W, DN_CONV_DIM)), const((1, LANES)), const((1, LANES)),
                  const((1, DN_HEAD_DIM)), const(lt.shape), const(sext.shape)],
        out_specs=[pl.BlockSpec((c, DN_WIDTH), lambda i: (i, 0)),
                   const((DN_HEADS, DN_HEAD_DIM, DN_HEAD_DIM)),
                   const((DN_CONV_W - 1, DN_CONV_DIM))],
        out_shape=[jax.ShapeDtypeStruct((t, DN_WIDTH), F32),
                   jax.ShapeDtypeStruct((DN_HEADS, DN_HEAD_DIM, DN_HEAD_DIM), F32),
                   jax.ShapeDtypeStruct((DN_CONV_W - 1, DN_CONV_DIM), F32)],
        scratch_shapes=[pltpu.VMEM((c + 8, DN_CONV_DIM), F32),
                        pltpu.VMEM((DN_HEADS, DN_HEAD_DIM, DN_HEAD_DIM), F32)],
        compiler_params=_params(("arbitrary",)),
        name="deltanet_prompt",
    )(qkv, z, ab, conv_w, alog, dtb, ng, lt, sext)


def _dn_sample_kernel(qkv_ref, z_ref, ab_ref, cprev_ref, s_ref, cw_ref, alog_ref, dtb_ref, ng_ref,
                      lt_ref, sext_ref, o_ref, s_out_ref, conv_out_ref, xp_scr, *, t):
    c = DN_CHUNK
    dh = DN_HEAD_DIM
    nreq = c // t
    x3 = qkv_ref[...]
    xp_scr[:, 8 - (DN_CONV_W - 1):8, :] = cprev_ref[...]
    xp_scr[:, 8:8 + t, :] = x3
    cw = cw_ref[...]
    y = xp_scr[:, 5:5 + t, :] * cw[0:1, :]
    y = y + xp_scr[:, 6:6 + t, :] * cw[1:2, :]
    y = y + xp_scr[:, 7:7 + t, :] * cw[2:3, :]
    y = y + x3 * cw[3:4, :]
    conv_out_ref[...] = xp_scr[:, 5 + t:8 + t, :]
    qkv = _silu(y).reshape(c, DN_CONV_DIM)

    g_all, beta_all = _dn_gates(ab_ref[...], alog_ref[...], dtb_ref[...])
    ng = ng_ref[...]
    rowid = lax.broadcasted_iota(jnp.int32, (c, 1), 0)
    local = _dn_local(_dn_items(qkv, g_all, beta_all, 0), lt_ref[...], sext_ref[...], t)
    for h in range(DN_HEADS):
        u, w, attn, q_dec, k_dec, tot = local[h]
        ws, qs = [], []
        for r in range(nreq):
            s = s_ref[r, h]
            ws.append(_mm1(w, s)[r * t:(r + 1) * t, :])
            qs.append(_mm1(q_dec, s)[r * t:(r + 1) * t, :])
        v_new = u - jnp.concatenate(ws, axis=0)
        o = jnp.concatenate(qs, axis=0) + _mm1(attn, v_new)
        for r in range(nreq):
            in_req = jnp.logical_and(rowid >= r * t, rowid < (r + 1) * t)
            kd_r = jnp.where(in_req, k_dec, 0.0)
            s_out_ref[r, h] = (s_ref[r, h] * jnp.exp(tot[r * t:r * t + 1, :])
                               + _mm1(kd_r, v_new, TN))
        o_ref[:, h * dh:(h + 1) * dh] = _dn_out(o, z_ref[:, h * dh:(h + 1) * dh], ng)


def _dn_sample(qkv, z, ab, conv_prev, ssm_prev, conv_w, alog, dtb, ng, t):
    n = qkv.shape[0]
    b = n // t
    c = DN_CHUNK
    assert t == 8 and c % t == 0 and b % (c // t) == 0
    nreq = c // t
    lt, sext = _dn_masks(t)
    const = lambda shape: pl.BlockSpec(shape, lambda i: tuple(0 for _ in shape))
    return pl.pallas_call(
        functools.partial(_dn_sample_kernel, t=t),
        grid=(b // nreq,),
        in_specs=[pl.BlockSpec((nreq, t, DN_CONV_DIM), lambda i: (i, 0, 0)),
                  pl.BlockSpec((c, DN_WIDTH), lambda i: (i, 0)),
                  pl.BlockSpec((c, LANES), lambda i: (i, 0)),
                  pl.BlockSpec((nreq, DN_CONV_W - 1, DN_CONV_DIM), lambda i: (i, 0, 0)),
                  pl.BlockSpec((nreq, DN_HEADS, DN_HEAD_DIM, DN_HEAD_DIM), lambda i: (i, 0, 0, 0)),
                  const((DN_CONV_W, DN_CONV_DIM)), const((1, LANES)), const((1, LANES)),
                  const((1, DN_HEAD_DIM)), const(lt.shape), const(sext.shape)],
        out_specs=[pl.BlockSpec((c, DN_WIDTH), lambda i: (i, 0)),
                   pl.BlockSpec((nreq, DN_HEADS, DN_HEAD_DIM, DN_HEAD_DIM), lambda i: (i, 0, 0, 0)),
                   pl.BlockSpec((nreq, DN_CONV_W - 1, DN_CONV_DIM), lambda i: (i, 0, 0))],
        out_shape=[jax.ShapeDtypeStruct((n, DN_WIDTH), F32),
                   jax.ShapeDtypeStruct(ssm_prev.shape, F32),
                   jax.ShapeDtypeStruct(conv_prev.shape, F32)],
        scratch_shapes=[pltpu.VMEM((nreq, 8 + t, DN_CONV_DIM), F32)],
        compiler_params=_params(("arbitrary",)),
        name="deltanet_sample",
    )(qkv.reshape(b, t, DN_CONV_DIM), z, ab, conv_prev, ssm_prev, conv_w, alog, dtb, ng, lt, sext)


LOG2E = 1.4426950408889634


SOFTPLUS_CLAMP = 100.0


def _sb_softplus(z2, vis):
    sp = jnp.maximum(jnp.log2(1.0 + jnp.exp2(jnp.minimum(z2, SOFTPLUS_CLAMP))), z2)
    if vis is not None:
        sp = jnp.where(vis, sp, 0.0)
    return sp.astype(BF16)


def _sb_weights(z2, cum, carry, vis):
    w = jnp.exp2(z2 + cum + carry)
    if vis is not None:
        w = jnp.where(vis, w, 0.0)
    return w.astype(BF16)


def _neg_upper(tk):
    j = jnp.arange(tk)[:, None]
    s = jnp.arange(tk)[None, :]
    return -(j >= s).astype(BF16)


def _div(x, d):
    if d & (d - 1) == 0:
        return x >> (d.bit_length() - 1)
    return lax.div(x, jnp.int32(d))


SB_CHUNK_PAGES = 8


def _sb_kernel(pt_ref, bias_ref, q_ref, k_ref, v_ref, u_ref, g_ref,
               qs_ref, kn_ref, vn_ref, brow_ref, us_ref, gs_ref, poolk_ref, poolv_ref,
               o_ref, os_ref,
               acc_scr, carry_scr, z0_scr, z1_scr, w0_scr, w1_scr,
               kbuf, vbuf, sacc_scr, scarry_scr, cnt_scr, sem, *, tq, tk, n_pages):
    z_scr = (z0_scr, z1_scr)
    w_scr = (w0_scr, w1_scr)
    p = pl.program_id(0)
    i = pl.program_id(1)
    first_step = jnp.logical_and(p == 0, i == 0)
    last_step = jnp.logical_and(p == pl.num_programs(0) - 1, i == pl.num_programs(1) - 1)
    nsub = tq // tk
    chains = [(hh, r) for hh in range(2) for r in range(nsub)]
    nch = len(chains)
    n_tiles = nsub * (i + 1)
    lane = lax.broadcasted_iota(jnp.int32, (1, LANES), 1)
    head_lanes = [lane < SB_HEAD_DIM, lane >= SB_HEAD_DIM]
    qs = q_ref[...] * (SB_HEAD_DIM ** -0.5 * LOG2E)
    qh = {(hh, r): jnp.where(head_lanes[hh], qs[r * tk:(r + 1) * tk, :], 0.0).astype(BF16)
          for hh, r in chains}
    bias = [bias_ref[2 * p] * LOG2E, bias_ref[2 * p + 1] * LOG2E]
    uneg = u_ref[...]
    col_minus_row = (lax.broadcasted_iota(jnp.int32, (tk, tk), 1)
                     - lax.broadcasted_iota(jnp.int32, (tk, tk), 0))
    acc_scr[...] = jnp.zeros_like(acc_scr)
    carry_scr[...] = jnp.zeros_like(carry_scr)

    n_req, t, _ = qs_ref.shape
    page = kbuf.shape[-1]
    npg = SB_CHUNK_PAGES
    parts = n_pages // npg
    n_chunks = n_req * parts
    rows = SB_HEADS * t
    t_shift = t.bit_length() - 1
    head_shift = SB_HEAD_DIM.bit_length() - 1

    def page_copies(c):
        slot = c & 1
        r = _div(c, parts)
        part = c - r * parts
        copies = []
        for n in range(npg):
            pg = pt_ref[r * n_pages + (n_pages - 1 - (part * npg + n))]
            copies.append(pltpu.make_async_copy(poolk_ref.at[pg], kbuf.at[slot, n], sem.at[0, slot]))
            copies.append(pltpu.make_async_copy(poolv_ref.at[pg], vbuf.at[slot, n], sem.at[1, slot]))
        return copies

    def fetch_next_and_wait(c):
        @pl.when(c + 1 < n_chunks)
        def _():
            for cp in page_copies(c + 1):
                cp.start()

        for cp in page_copies(c):
            cp.wait()

    def sample_chunk(c):
        slot = c & 1
        r = _div(c, parts)
        first = (c - r * parts) == 0
        st = {}

        def logits():
            brow = brow_ref[...] * LOG2E
            row = lax.broadcasted_iota(jnp.int32, (rows, SB_WIDTH), 0)
            col = lax.broadcasted_iota(jnp.int32, (rows, SB_WIDTH), 1)
            q8 = qs_ref[r] * (SB_HEAD_DIM ** -0.5 * LOG2E)
            qt = jnp.concatenate([q8] * SB_HEADS, axis=0)
            qbd = jnp.where((row >> t_shift) == (col >> head_shift), qt, 0.0).astype(BF16)
            pad = jnp.zeros((page - t, SB_WIDTH), F32)
            kn = jnp.concatenate([kn_ref[r], pad], axis=0).astype(BF16)
            st["vn"] = jnp.concatenate([vn_ref[r], pad], axis=0).astype(BF16)
            zs = [_dot(qbd, kn, NT) + brow]
            for n in range(npg):
                zs.append(_dot(qbd, kbuf[slot, n].astype(BF16)) + brow)
            st["zs"] = zs

        def sums():
            krow = lax.broadcasted_iota(jnp.int32, (rows, page), 0)
            kcol = lax.broadcasted_iota(jnp.int32, (rows, page), 1)
            vis = kcol < jnp.where(first, krow & (t - 1), 0)
            st["vis"] = vis
            us = us_ref[...]
            zs = st["zs"]
            st["cums"] = ([_dot(_sb_softplus(zs[0], vis), us)]
                          + [_dot(_sb_softplus(z, None), us) for z in zs[1:]])

        def values():
            zs, cums = st["zs"], st["cums"]
            carry = jnp.where(first, 0.0, scarry_scr[...])
            acc = jnp.where(first, 0.0, sacc_scr[...])
            acc = acc + _dot(_sb_weights(zs[0], cums[0], carry, st["vis"]), st["vn"])
            carry = carry + cums[0][:, 0:1]
            for n in range(npg):
                w = _sb_weights(zs[n + 1], cums[n + 1], carry, None)
                acc = acc + _dot(w, vbuf[slot, n].astype(BF16), NT)
                carry = carry + cums[n + 1][:, 0:1]
            sacc_scr[...] = acc
            scarry_scr[...] = carry

        return logits, sums, values

    def sample_output(c):
        acc = sacc_scr[...]
        col = lax.broadcasted_iota(jnp.int32, (t, SB_WIDTH), 1)
        out = jnp.zeros((t, SB_WIDTH), F32)
        for h in range(SB_HEADS):
            out = out + jnp.where((col >> head_shift) == h, acc[h * t:(h + 1) * t, :], 0.0)
        sq = out * out
        ms = jnp.zeros((t, SB_WIDTH), F32)
        for h in range(SB_HEADS):
            in_h = (col >> head_shift) == h
            ms_h = jnp.sum(jnp.where(in_h, sq, 0.0), axis=-1, keepdims=True) * (1.0 / SB_HEAD_DIM)
            ms = jnp.where(in_h, ms_h, ms)
        os_ref[_div(c, parts)] = out * lax.rsqrt(ms + NORM_EPS) * gs_ref[...]

    @pl.when(first_step)
    def _():
        cnt_scr[0] = 0
        sacc_scr[...] = jnp.zeros_like(sacc_scr)
        scarry_scr[...] = jnp.zeros_like(scarry_scr)
        for cp in page_copies(0):
            cp.start()

    def key_tile(n):
        j = jnp.maximum(n_tiles - 1 - n, 0)
        return j, pl.multiple_of(j * tk, tk)

    def logits(n, slot, ci):
        _, start = key_tile(n)
        kt = k_ref[:, pl.ds(start, tk)]
        z_scr[slot][ci] = _dot(qh[chains[ci]], kt) + bias[chains[ci][0]]

    def values(n, slot, ci):
        _, start = key_tile(n)
        vt = v_ref[:, pl.ds(start, tk)]
        acc_scr[ci] += _dot(w_scr[slot][ci], vt, NT)

    def visit_phases(n, slot, masked, with_values):
        j, _ = key_tile(n)
        if masked:
            vis = [col_minus_row < (i * tq + r * tk - j * tk) for _, r in chains]
        else:
            vis = [None] * nch
        zs, sps, cums = [], [], []

        def phase1():
            for ci in range(nch):
                zs.append(z_scr[slot][ci])
                sps.append(_sb_softplus(zs[ci], vis[ci]))
                logits(n + 1, 1 - slot, ci)

        def phase2():
            for ci in range(nch):
                cums.append(_dot(sps[ci], uneg))
                if with_values:
                    values(n - 1, 1 - slot, ci)

        def phase3():
            for ci in range(nch):
                w_scr[slot][ci] = _sb_weights(zs[ci], cums[ci], carry_scr[ci], vis[ci])
                carry_scr[ci] += cums[ci][:, 0:1]

        return phase1, phase2, phase3

    def visit(n, slot, masked, with_values):
        for phase in visit_phases(n, slot, masked, with_values):
            phase()

    for ci in range(nch):
        logits(0, 0, ci)
    for n in range(nsub):
        visit(n, n % 2, True, n > 0)

    n_trips = (n_tiles - nsub) // 2
    c0 = cnt_scr[0]
    n_fused = jnp.minimum(n_trips, n_chunks - c0)

    def fused_trip(m, carry):
        c = c0 + m
        fetch_next_and_wait(c)
        s_logits, s_sums, s_values = sample_chunk(c)
        n = nsub + 2 * m
        a1, a2, a3 = visit_phases(n, nsub % 2, False, True)
        b1, b2, b3 = visit_phases(n + 1, (nsub + 1) % 2, False, True)
        sample_output(jnp.maximum(c - 1, 0))
        s_logits()
        a1()
        a2()
        s_sums()
        a3()
        b1()
        b2()
        s_values()
        b3()
        return carry

    def plain_trip(m, carry):
        n = nsub + 2 * m
        visit(n, nsub % 2, False, True)
        visit(n + 1, (nsub + 1) % 2, False, True)
        return carry

    lax.fori_loop(0, n_fused, fused_trip, 0)
    lax.fori_loop(n_fused, n_trips, plain_trip, 0)
    cnt_scr[0] = c0 + n_fused
    for ci in range(nch):
        values(n_tiles - 1, (nsub - 1) % 2, ci)
    for r in range(nsub):
        out = jnp.where(head_lanes[0], acc_scr[chains.index((0, r))], acc_scr[chains.index((1, r))])
        sq = out * out
        ms = [jnp.sum(jnp.where(m, sq, 0.0), axis=-1, keepdims=True) * (1.0 / SB_HEAD_DIM)
              for m in head_lanes]
        ms = jnp.where(head_lanes[0], ms[0], ms[1])
        o_ref[r * tk:(r + 1) * tk, :] = out * lax.rsqrt(ms + NORM_EPS) * g_ref[...]

    @pl.when(last_step)
    def _():
        def rest(c, carry):
            fetch_next_and_wait(c)
            sample_output(jnp.maximum(c - 1, 0))
            for stage in sample_chunk(c):
                stage()
            return carry

        lax.fori_loop(c0 + n_fused, n_chunks, rest, 0)
        sample_output(n_chunks - 1)


def _sb_attention(q, kt_bf, vt_bf, q_s, k_new, v_new, pool_k, pool_v, page_table, bias, g_sb,
                  tq=512, tk=256):
    t_len = q.shape[0]
    tq = min(tq, t_len)
    assert t_len % tq == 0 and tq % (2 * tk) == 0
    b, t, _ = q_s.shape
    n_pages = page_table.shape[1]
    page = pool_k.shape[2]
    npg = SB_CHUNK_PAGES
    assert n_pages % npg == 0 and page == LANES and t == 8
    npairs = SB_WIDTH // LANES
    nchains = 2 * (tq // tk)
    rows = SB_HEADS * t
    brow = jnp.broadcast_to(jnp.repeat(bias, t)[:, None], (rows, LANES))
    g2 = jnp.tile(g_sb, LANES // SB_HEAD_DIM).reshape(1, LANES)
    g8 = jnp.tile(g_sb, SB_HEADS).reshape(1, SB_WIDTH)
    smem = pl.BlockSpec(memory_space=pltpu.SMEM)
    hbm = pl.BlockSpec(memory_space=pl.ANY)
    whole = lambda a: pl.BlockSpec(a.shape, lambda p, i: (0,) * a.ndim, pipeline_mode=pl.Buffered(1))
    return pl.pallas_call(
        functools.partial(_sb_kernel, tq=tq, tk=tk, n_pages=n_pages),
        grid=(npairs, t_len // tq),
        in_specs=[smem, smem,
                  pl.BlockSpec((tq, LANES), lambda p, i: (i, p)),
                  pl.BlockSpec((LANES, t_len), lambda p, i: (p, 0)),
                  pl.BlockSpec((LANES, t_len), lambda p, i: (p, 0)),
                  pl.BlockSpec((tk, tk), lambda p, i: (0, 0)),
                  pl.BlockSpec((1, LANES), lambda p, i: (0, 0)),
                  whole(q_s), whole(k_new), whole(v_new),
                  pl.BlockSpec((rows, LANES), lambda p, i: (0, 0)),
                  pl.BlockSpec((page, page), lambda p, i: (0, 0)),
                  pl.BlockSpec((1, SB_WIDTH), lambda p, i: (0, 0)),
                  hbm, hbm],
        out_specs=[pl.BlockSpec((tq, LANES), lambda p, i: (i, p)),
                   pl.BlockSpec((b, t, SB_WIDTH), lambda p, i: (0, 0, 0))],
        out_shape=[jax.ShapeDtypeStruct((t_len, SB_WIDTH), F32),
                   jax.ShapeDtypeStruct((b, t, SB_WIDTH), F32)],
        scratch_shapes=[pltpu.VMEM((nchains, tk, LANES), F32),
                        pltpu.VMEM((nchains, tk, 1), F32),
                        pltpu.VMEM((nchains, tk, tk), F32), pltpu.VMEM((nchains, tk, tk), F32),
                        pltpu.VMEM((nchains, tk, tk), BF16), pltpu.VMEM((nchains, tk, tk), BF16),
                        pltpu.VMEM((2, npg, SB_WIDTH, page), F32),
                        pltpu.VMEM((2, npg, SB_WIDTH, page), F32),
                        pltpu.VMEM((rows, SB_WIDTH), F32),
                        pltpu.VMEM((rows, 1), F32),
                        pltpu.SMEM((1,), jnp.int32),
                        pltpu.SemaphoreType.DMA((2, 2))],
        compiler_params=_params(("arbitrary", "arbitrary")),
        name="sb_attention",
    )(page_table.reshape(-1), bias, q, kt_bf, vt_bf, _neg_upper(tk), g2,
      q_s, k_new, v_new, brow, _neg_upper(page), g8, pool_k, pool_v)


def _split_w_in(w_in):
    d = w_in.shape[0]
    o_ab = DN_CONV_DIM + DN_WIDTH
    o_q = o_ab + 2 * DN_HEADS
    o_k = o_q + SB_WIDTH
    o_v = o_k + SB_WIDTH
    pad = jnp.zeros((d, LANES - 2 * DN_HEADS), w_in.dtype)
    w_r = jnp.concatenate([w_in[:, 0:o_ab], w_in[:, o_q:o_k], w_in[:, o_ab:o_q], pad], axis=1)
    return w_r.astype(BF16), w_in[:, o_k:o_v].astype(BF16), w_in[:, o_v:].astype(BF16)


def _heads_last(xt, n):
    return jnp.transpose(xt.reshape(SB_HEADS, SB_HEAD_DIM, n), (2, 0, 1))


def _lane_row(v):
    return jnp.pad(v.astype(F32), (0, LANES - v.shape[0])).reshape(1, LANES)


def kernel(x_prompt, x_sample, c_prompt, c_sample, cache_sb_k, cache_sb_v, page_table, state_dn_ssm, state_dn_conv, w_ada, b_ada, g_ffn1, w1_gate, w1_up, w1_down, g_mix, w_in, dn_conv_w, dn_a_log, dn_dt_bias, dn_norm_g, sb_norm_g, sb_bias, w_out, g_ffn2, w2_gate, w2_up, w2_down, g_final):
    depth = w_ada.shape[0]
    bsz, seq, d = x_prompt.shape
    dec_b, dec_t, _ = x_sample.shape
    assert bsz == 1
    xp, xs = x_prompt, x_sample
    c_all = jnp.concatenate([c_prompt, c_sample], axis=0)
    n_c = c_all.shape[0]
    c_all = jnp.pad(c_all, ((0, (-n_c) % 8), (0, 0)))
    outs = {name: [] for name in ("kp", "vp", "ssmp", "convp", "ks", "vs", "ssms", "convs")}
    for l in range(depth):
        last = l == depth - 1
        mods = _adaln(c_all, w_ada[l].astype(BF16), b_ada[l]).reshape(-1, N_MOD, d)
        mp = [mods[0:bsz, i:i + 1, :] for i in range(N_MOD)]
        ms = [mods[bsz:bsz + dec_b, i:i + 1, :] for i in range(N_MOD)]
        w1 = (w1_gate[l].astype(BF16), w1_up[l].astype(BF16), w1_down[l].astype(BF16))
        w2 = (w2_gate[l].astype(BF16), w2_up[l].astype(BF16), w2_down[l].astype(BF16))
        w_in_r, w_k, w_v = _split_w_in(w_in[l])
        w_out_b = w_out[l].astype(BF16)
        alog = _lane_row(dn_a_log[l])
        dtb = _lane_row(dn_dt_bias[l])
        ng = dn_norm_g[l].reshape(1, DN_HEAD_DIM).astype(F32)
        g_sb = sb_norm_g[l].astype(F32)
        bias = sb_bias[l].astype(F32)

        xp = _ffn(xp, mp[0], mp[1], mp[2], g_ffn1[l], *w1)
        qkv, z, q_p, ab, kt, vt, kt_bf, vt_bf = _proj(xp, mp[3], mp[4], g_mix[l], w_in_r, w_k.T, w_v.T, True)
        o_dn_p, ssm_p, conv_p = _dn_prompt(qkv, z, ab, dn_conv_w[l], alog, dtb, ng)
        outs["kp"].append(_heads_last(kt, seq).reshape(bsz, seq, SB_HEADS, SB_HEAD_DIM))
        outs["vp"].append(_heads_last(vt, seq).reshape(bsz, seq, SB_HEADS, SB_HEAD_DIM))
        outs["ssmp"].append(ssm_p.reshape(bsz, DN_HEADS, DN_HEAD_DIM, DN_HEAD_DIM))
        outs["convp"].append(conv_p.reshape(bsz, DN_CONV_W - 1, DN_CONV_DIM))

        xs = _ffn(xs, ms[0], ms[1], ms[2], g_ffn1[l], *w1)
        qkv, z, q, ab, k, v = _proj(xs, ms[3], ms[4], g_mix[l], w_in_r, w_k, w_v, False)
        o_dn, ssm_s, conv_s = _dn_sample(qkv, z, ab, state_dn_conv[l], state_dn_ssm[l],
                                         dn_conv_w[l], alog, dtb, ng, dec_t)
        n_pool, page = cache_sb_k.shape[1], cache_sb_k.shape[2]
        pool_k = jnp.transpose(cache_sb_k[l], (0, 2, 3, 1)).reshape(n_pool, SB_WIDTH, page)
        pool_v = jnp.transpose(cache_sb_v[l], (0, 2, 3, 1)).reshape(n_pool, SB_WIDTH, page)
        o_sb_p, o_sb_s = _sb_attention(q_p, kt_bf, vt_bf, q.reshape(dec_b, dec_t, SB_WIDTH),
                                       k.reshape(dec_b, dec_t, SB_WIDTH), v.reshape(dec_b, dec_t, SB_WIDTH),
                                       pool_k, pool_v, page_table, bias, g_sb)
        xp = _outproj(xp, mp[5], o_dn_p, o_sb_p, w_out_b)
        xp = _ffn(xp, mp[6], mp[7], mp[8], g_ffn2[l], *w2, gfin=g_final if last else None)
        xs = _outproj(xs, ms[5], o_dn, o_sb_s.reshape(dec_b * dec_t, SB_WIDTH), w_out_b)
        xs = _ffn(xs, ms[6], ms[7], ms[8], g_ffn2[l], *w2, gfin=g_final if last else None)
        outs["ks"].append(k.reshape(dec_b, dec_t, SB_HEADS, SB_HEAD_DIM))
        outs["vs"].append(v.reshape(dec_b, dec_t, SB_HEADS, SB_HEAD_DIM))
        outs["ssms"].append(ssm_s)
        outs["convs"].append(conv_s)
    if depth == 0:
        raise ValueError("depth must be positive")
    return (xp, xs,
            jnp.stack(outs["kp"]), jnp.stack(outs["vp"]), jnp.stack(outs["ssmp"]), jnp.stack(outs["convp"]),
            jnp.stack(outs["ks"]), jnp.stack(outs["vs"]), jnp.stack(outs["ssms"]), jnp.stack(outs["convs"]))
```

```python
import functools

import jax
import jax.numpy as jnp
from jax import lax
from jax.experimental import pallas as pl
from jax.experimental.pallas import tpu as pltpu

F32 = jnp.float32
BF16 = jnp.bfloat16

DN_HEADS = 4
DN_HEAD_DIM = 128
DN_WIDTH = DN_HEADS * DN_HEAD_DIM
DN_CONV_W = 4
DN_CONV_DIM = 3 * DN_WIDTH
DN_CHUNK = 64
SB_HEADS = 8
SB_HEAD_DIM = 64
SB_WIDTH = SB_HEADS * SB_HEAD_DIM
N_MOD = 9
NORM_EPS = 1e-6
L2_EPS = 1e-6
LANES = 128

NN = (((1,), (0,)), ((), ()))
NT = (((1,), (1,)), ((), ()))
TN = (((0,), (0,)), ((), ()))

VMEM_LIMIT = 56 * 1024 * 1024


def _params(sem):
    return pltpu.CompilerParams(dimension_semantics=sem, vmem_limit_bytes=VMEM_LIMIT)


def _dot(a, b, dims=NN):
    return lax.dot_general(a, b, dims, preferred_element_type=F32)


def _split2(x):
    hi = x.astype(BF16)
    lo = (x - hi.astype(F32)).astype(BF16)
    return hi, lo


def _split3(x):
    hi = x.astype(BF16)
    r = x - hi.astype(F32)
    mid = r.astype(BF16)
    lo = (r - mid.astype(F32)).astype(BF16)
    return hi, mid, lo


def _mm3(a, b, dims=NN):
    ah, al = _split2(a)
    bh, bl = _split2(b)
    return _dot(ah, bh, dims) + (_dot(ah, bl, dims) + _dot(al, bh, dims))


def _mm1(a, b, dims=NN):
    return _dot(a.astype(BF16), b.astype(BF16), dims)


def _mm_mask_left(m01, x):
    hi, mid, lo = _split3(x)
    return _dot(m01, hi) + (_dot(m01, mid) + _dot(m01, lo))


def _sigmoid(x):
    return 1.0 / (1.0 + jnp.exp(-x))


def _silu(x):
    return x * _sigmoid(x)


def _softplus(x):
    return jnp.maximum(x, 0.0) + jnp.log1p(jnp.exp(-jnp.abs(x)))


def _modnorm(x, gain, shift, scale):
    y = x * lax.rsqrt(jnp.mean(x * x, axis=-1, keepdims=True) + NORM_EPS)
    return (y * gain) * (1.0 + scale) + shift


def _adaln_kernel(c_ref, w_ref, b_ref, o_ref):
    c = _silu(c_ref[...]).astype(BF16)
    o_ref[...] = _dot(c, w_ref[...]) + b_ref[...]


def _adaln(c, w_bf, b):
    n, d = c.shape
    cols = w_bf.shape[1]
    tn = d
    return pl.pallas_call(
        _adaln_kernel,
        grid=(cols // tn,),
        in_specs=[pl.BlockSpec((n, d), lambda j: (0, 0)),
                  pl.BlockSpec((d, tn), lambda j: (0, j)),
                  pl.BlockSpec((1, tn), lambda j: (0, j))],
        out_specs=pl.BlockSpec((n, tn), lambda j: (0, j)),
        out_shape=jax.ShapeDtypeStruct((n, cols), F32),
        compiler_params=_params(("arbitrary",)),
        name="adaln",
    )(c, w_bf, b.reshape(1, cols))


def _row_blocks(g, r, target):
    if r >= target:
        assert r % target == 0
        return 1, target
    gb = min(g, target // r)
    assert g % gb == 0
    return gb, r


MXU_WIDTH = 256


def _ffn_kernel(x_ref, sh_ref, sc_ref, gt_ref, gain_ref, wg_ref, wu_ref, wd_ref, *rest, final_norm):
    if final_norm:
        gfin_ref, o_ref, acc_scr = rest
    else:
        o_ref, acc_scr = rest
    rows, d = acc_scr.shape
    x = x_ref[...]
    h = _modnorm(x, gain_ref[...], sh_ref[...], sc_ref[...]).reshape(rows, d).astype(BF16)
    dff = wg_ref.shape[1]
    for f in range(dff // MXU_WIDTH):
        cols = slice(f * MXU_WIDTH, (f + 1) * MXU_WIDTH)
        a = _dot(h, wg_ref[:, cols])
        b = _dot(h, wu_ref[:, cols])
        part = _dot((_silu(a) * b).astype(BF16), wd_ref[cols, :])
        if f == 0:
            acc_scr[...] = part
        else:
            acc_scr[...] += part
    y = x + 0.5 * gt_ref[...] * acc_scr[...].reshape(x.shape)
    if final_norm:
        y = y * lax.rsqrt(jnp.mean(y * y, axis=-1, keepdims=True) + NORM_EPS) * gfin_ref[...]
    o_ref[...] = y


def _ffn(x, shift, scale, gate, gain, wg, wu, wd, gfin=None, rows_target=512):
    g, r, d = x.shape
    dff = wg.shape[1]
    assert dff % MXU_WIDTH == 0
    gb, rb = _row_blocks(g, r, rows_target)
    rows = gb * rb
    mod_spec = pl.BlockSpec((gb, 1, d), lambda i, j: (i, 0, 0))
    vec_spec = pl.BlockSpec((1, d), lambda i, j: (0, 0))
    resident = lambda a: pl.BlockSpec(a.shape, lambda i, j: (0, 0), pipeline_mode=pl.Buffered(1))
    in_specs = [pl.BlockSpec((gb, rb, d), lambda i, j: (i, j, 0)),
                mod_spec, mod_spec, mod_spec, vec_spec,
                resident(wg), resident(wu), resident(wd)]
    args = [x, shift, scale, gate, gain.reshape(1, d), wg, wu, wd]
    if gfin is not None:
        in_specs.append(vec_spec)
        args.append(gfin.reshape(1, d))
    return pl.pallas_call(
        functools.partial(_ffn_kernel, final_norm=gfin is not None),
        grid=(g // gb, r // rb),
        in_specs=in_specs,
        out_specs=pl.BlockSpec((gb, rb, d), lambda i, j: (i, j, 0)),
        out_shape=jax.ShapeDtypeStruct(x.shape, F32),
        scratch_shapes=[pltpu.VMEM((rows, d), F32)],
        compiler_params=_params(("arbitrary", "arbitrary")),
        name="ffn",
    )(*args)


PROJ_COLS = (DN_CONV_DIM, DN_WIDTH, SB_WIDTH, LANES)


def _proj_kernel(x_ref, sh_ref, sc_ref, gain_ref, w_ref, wk_ref, wv_ref, *o_refs, transposed):
    rows = o_refs[0].shape[0]
    h = _modnorm(x_ref[...], gain_ref[...], sh_ref[...], sc_ref[...])
    h = h.reshape(rows, h.shape[-1]).astype(BF16)
    off = 0
    for o_ref, cols in zip(o_refs[0:len(PROJ_COLS)], PROJ_COLS):
        o_ref[...] = _dot(h, w_ref[:, off:off + cols])
        off += cols
    if transposed:
        k_ref, v_ref, kb_ref, vb_ref = o_refs[len(PROJ_COLS):]
        kt = _dot(wk_ref[...], h, NT)
        vt = _dot(wv_ref[...], h, NT)
        k_ref[...] = kt
        v_ref[...] = vt
        kb_ref[...] = kt.astype(BF16)
        vb_ref[...] = vt.astype(BF16)
    else:
        k_ref, v_ref = o_refs[len(PROJ_COLS):]
        k_ref[...] = _dot(h, wk_ref[...])
        v_ref[...] = _dot(h, wv_ref[...])


def _proj(x, shift, scale, gain, w_r, w_k, w_v, transposed, rows_target=512):
    g, r, d = x.shape
    gb, rb = _row_blocks(g, r, rows_target)
    rows = gb * rb
    nj = r // rb
    n = g * r
    mod_spec = pl.BlockSpec((gb, 1, d), lambda i, j: (i, 0, 0))
    const = lambda a: pl.BlockSpec(a.shape, lambda i, j: (0, 0))
    out_specs = [pl.BlockSpec((rows, c), lambda i, j: (i * nj + j, 0)) for c in PROJ_COLS]
    out_shape = [jax.ShapeDtypeStruct((n, c), F32) for c in PROJ_COLS]
    if transposed:
        out_specs += [pl.BlockSpec((SB_WIDTH, rows), lambda i, j: (0, i * nj + j))] * 4
        out_shape += [jax.ShapeDtypeStruct((SB_WIDTH, n), dt) for dt in (F32, F32, BF16, BF16)]
    else:
        out_specs += [pl.BlockSpec((rows, SB_WIDTH), lambda i, j: (i * nj + j, 0))] * 2
        out_shape += [jax.ShapeDtypeStruct((n, SB_WIDTH), F32)] * 2
    return pl.pallas_call(
        functools.partial(_proj_kernel, transposed=transposed),
        grid=(g // gb, nj),
        in_specs=[pl.BlockSpec((gb, rb, d), lambda i, j: (i, j, 0)),
                  mod_spec, mod_spec,
                  pl.BlockSpec((1, d), lambda i, j: (0, 0)),
                  const(w_r), const(w_k), const(w_v)],
        out_specs=out_specs,
        out_shape=out_shape,
        compiler_params=_params(("arbitrary", "arbitrary")),
        name="proj_in",
    )(x, shift, scale, gain.reshape(1, d), w_r, w_k, w_v)


def _outproj_kernel(x_ref, gt_ref, odn_ref, osb_ref, w_ref, o_ref):
    x = x_ref[...]
    mixed = (_dot(odn_ref[...].astype(BF16), w_ref[0:DN_WIDTH, :])
             + _dot(osb_ref[...].astype(BF16), w_ref[DN_WIDTH:DN_WIDTH + SB_WIDTH, :]))
    o_ref[...] = x + gt_ref[...] * mixed.reshape(x.shape)


def _outproj(x, gate, o_dn, o_sb, w_out, rows_target=512):
    g, r, d = x.shape
    gb, rb = _row_blocks(g, r, rows_target)
    rows = gb * rb
    nj = r // rb
    return pl.pallas_call(
        _outproj_kernel,
        grid=(g // gb, nj),
        in_specs=[pl.BlockSpec((gb, rb, d), lambda i, j: (i, j, 0)),
                  pl.BlockSpec((gb, 1, d), lambda i, j: (i, 0, 0)),
                  pl.BlockSpec((rows, DN_WIDTH), lambda i, j: (i * nj + j, 0)),
                  pl.BlockSpec((rows, SB_WIDTH), lambda i, j: (i * nj + j, 0)),
                  pl.BlockSpec(w_out.shape, lambda i, j: (0, 0))],
        out_specs=pl.BlockSpec((gb, rb, d), lambda i, j: (i, j, 0)),
        out_shape=jax.ShapeDtypeStruct(x.shape, F32),
        compiler_params=_params(("arbitrary", "arbitrary")),
        name="proj_out",
    )(x, gate, o_dn, o_sb, w_out)


def _dn_masks(group):
    c = DN_CHUNK
    i = jnp.arange(c)[:, None]
    m = jnp.arange(c)[None, :]
    same = (i // group) == (m // group)
    lower = jnp.logical_and(m <= i, same)
    lt = jnp.concatenate([lower, same], axis=0).astype(BF16)
    sext = jnp.concatenate([(i > m), jnp.ones((c, c), bool)], axis=1).astype(F32)
    return lt, sext


def _dn_gates(ab, alog, dtb):
    g = -jnp.exp(alog) * _softplus(ab + dtb)
    beta = _sigmoid(ab)
    return g, beta


def _dn_local(items, lt, sext, group):
    c = DN_CHUNK
    row = lax.broadcasted_iota(jnp.int32, (c, c), 0)
    col = lax.broadcasted_iota(jnp.int32, (c, c), 1)
    lower = col <= row
    strict = col < row
    if group < c:
        shift = group.bit_length() - 1
        same = (row >> shift) == (col >> shift)
        lower = jnp.logical_and(lower, same)
        strict = jnp.logical_and(strict, same)
    eye = (row == col).astype(F32)
    n_sq = max(1, (min(group, c) - 1).bit_length() - 1)

    qs, ks, kbs, dms, gams, tots = [], [], [], [], [], []
    for q_raw, k_raw, _, g_col, beta_col in items:
        qs.append(q_raw * lax.rsqrt(jnp.sum(q_raw * q_raw, axis=-1, keepdims=True) + L2_EPS)
                  * (DN_HEAD_DIM ** -0.5))
        k = k_raw * lax.rsqrt(jnp.sum(k_raw * k_raw, axis=-1, keepdims=True) + L2_EPS)
        ks.append(k)
        kbs.append(k * beta_col)
    for _, _, _, g_col, _ in items:
        r = _mm_mask_left(lt, g_col * sext)
        diff = r[0:c, 0:c]
        gams.append(r[0:c, c:c + 1])
        tots.append(r[c:2 * c, c:c + 1])
        dms.append(jnp.where(lower, jnp.exp(jnp.where(lower, diff, 0.0)), 0.0))
    a_s = [jnp.where(strict, _mm1(kb, k, NT) * dm, 0.0) for kb, k, dm in zip(kbs, ks, dms)]
    attns = [_mm1(q, k, NT) * dm for q, k, dm in zip(qs, ks, dms)]
    ps = [eye - a for a in a_s]
    xs = a_s
    for _ in range(n_sq):
        xs = [_mm3(x, x) for x in xs]
        ps = [p + _mm3(p, x) for p, x in zip(ps, xs)]
    egs = [jnp.exp(gam) for gam in gams]
    uws = [_mm1(p, jnp.concatenate([it[2] * it[4], kb * eg], axis=1))
           for p, it, kb, eg in zip(ps, items, kbs, egs)]
    out = []
    for q, k, uw, attn, eg, gam, tot in zip(qs, ks, uws, attns, egs, gams, tots):
        out.append((uw[:, 0:DN_HEAD_DIM], uw[:, DN_HEAD_DIM:2 * DN_HEAD_DIM], attn,
                    q * eg, k * jnp.exp(tot - gam), tot))
    return out


def _dn_items(qkv, g_all, beta_all, r0):
    c = DN_CHUNK
    dh = DN_HEAD_DIM
    items = []
    for h in range(DN_HEADS):
        items.append((qkv[r0:r0 + c, h * dh:(h + 1) * dh],
                      qkv[r0:r0 + c, DN_WIDTH + h * dh:DN_WIDTH + (h + 1) * dh],
                      qkv[r0:r0 + c, 2 * DN_WIDTH + h * dh:2 * DN_WIDTH + (h + 1) * dh],
                      g_all[r0:r0 + c, h:h + 1],
                      beta_all[r0:r0 + c, DN_HEADS + h:DN_HEADS + h + 1]))
    return items


def _dn_out(o, z, ng):
    y = o * lax.rsqrt(jnp.mean(o * o, axis=-1, keepdims=True) + NORM_EPS) * ng
    return y * _silu(z)


def _dn_prompt_kernel(qkv_ref, z_ref, ab_ref, cw_ref, alog_ref, dtb_ref, ng_ref, lt_ref, sext_ref,
                      o_ref, s_out_ref, conv_out_ref, xp_scr, s_scr):
    c = DN_CHUNK
    dh = DN_HEAD_DIM
    tb = qkv_ref.shape[0]
    step = pl.program_id(0)

    @pl.when(step == 0)
    def _():
        xp_scr[0:8, :] = jnp.zeros((8, DN_CONV_DIM), F32)
        s_scr[...] = jnp.zeros_like(s_scr)

    x = qkv_ref[...]
    xp_scr[8:8 + tb, :] = x
    cw = cw_ref[...]
    y = xp_scr[5:5 + tb, :] * cw[0:1, :]
    y = y + xp_scr[6:6 + tb, :] * cw[1:2, :]
    y = y + xp_scr[7:7 + tb, :] * cw[2:3, :]
    y = y + x * cw[3:4, :]
    conv_out_ref[...] = xp_scr[5 + tb:8 + tb, :]
    xp_scr[0:8, :] = xp_scr[tb:tb + 8, :]
    qkv = _silu(y)

    g_all, beta_all = _dn_gates(ab_ref[...], alog_ref[...], dtb_ref[...])
    ng = ng_ref[...]
    items = []
    for n in range(tb // c):
        items += _dn_items(qkv, g_all, beta_all, n * c)
    local = _dn_local(items, lt_ref[...], sext_ref[...], c)
    s = [s_scr[h] for h in range(DN_HEADS)]
    for n in range(tb // c):
        loc = local[n * DN_HEADS:(n + 1) * DN_HEADS]
        v_new = [u - _mm1(w, s[h]) for h, (u, w, _, _, _, _) in enumerate(loc)]
        o = [_mm1(q_dec, s[h]) + _mm1(attn, v_new[h]) for h, (_, _, attn, q_dec, _, _) in enumerate(loc)]
        s = [s[h] * jnp.exp(tot[0:1, :]) + _mm1(k_dec, v_new[h], TN)
             for h, (_, _, _, _, k_dec, tot) in enumerate(loc)]
        for h in range(DN_HEADS):
            o_ref[n * c:(n + 1) * c, h * dh:(h + 1) * dh] = _dn_out(
                o[h], z_ref[n * c:(n + 1) * c, h * dh:(h + 1) * dh], ng)
    for h in range(DN_HEADS):
        s_scr[h] = s[h]
        s_out_ref[h] = s[h]


def _dn_prompt(qkv, z, ab, conv_w, alog, dtb, ng, chunks_per_step=4):
    t = qkv.shape[0]
    c = DN_CHUNK * min(chunks_per_step, t // DN_CHUNK)
    assert t % c == 0
    lt, sext = _dn_masks(DN_CHUNK)
    const = lambda shape: pl.BlockSpec(shape, lambda i: tuple(0 for _ in shape))
    return pl.pallas_call(
        _dn_prompt_kernel,
        grid=(t // c,),
        in_specs=[pl.BlockSpec((c, DN_CONV_DIM), lambda i: (i, 0)),
                  pl.BlockSpec((c, DN_WIDTH), lambda i: (i, 0)),
                  pl.BlockSpec((c, LANES), lambda i: (i, 0)),
                  const((DN_CONV_W, DN_CONV_DIM)), const((1, LANES)), const((1, LANES)),
                  const((1, DN_HEAD_DIM)), const(lt.shape), const(sext.shape)],
        out_specs=[pl.BlockSpec((c, DN_WIDTH), lambda i: (i, 0)),
                   const((DN_HEADS, DN_HEAD_DIM, DN_HEAD_DIM)),
                   const((DN_CONV_W - 1, DN_CONV_DIM))],
        out_shape=[jax.ShapeDtypeStruct((t, DN_WIDTH), F32),
                   jax.ShapeDtypeStruct((DN_HEADS, DN_HEAD_DIM, DN_HEAD_DIM), F32),
                   jax.ShapeDtypeStruct((DN_CONV_W - 1, DN_CONV_DIM), F32)],
        scratch_shapes=[pltpu.VMEM((c + 8, DN_CONV_DIM), F32),
                        pltpu.VMEM((DN_HEADS, DN_HEAD_DIM, DN_HEAD_DIM), F32)],
        compiler_params=_params(("arbitrary",)),
        name="deltanet_prompt",
    )(qkv, z, ab, conv_w, alog, dtb, ng, lt, sext)


def _dn_sample_kernel(qkv_ref, z_ref, ab_ref, cprev_ref, s_ref, cw_ref, alog_ref, dtb_ref, ng_ref,
                      lt_ref, sext_ref, o_ref, s_out_ref, conv_out_ref, xp_scr, *, t):
    c = DN_CHUNK
    dh = DN_HEAD_DIM
    nreq = c // t
    x3 = qkv_ref[...]
    xp_scr[:, 8 - (DN_CONV_W - 1):8, :] = cprev_ref[...]
    xp_scr[:, 8:8 + t, :] = x3
    cw = cw_ref[...]
    y = xp_scr[:, 5:5 + t, :] * cw[0:1, :]
    y = y + xp_scr[:, 6:6 + t, :] * cw[1:2, :]
    y = y + xp_scr[:, 7:7 + t, :] * cw[2:3, :]
    y = y + x3 * cw[3:4, :]
    conv_out_ref[...] = xp_scr[:, 5 + t:8 + t, :]
    qkv = _silu(y).reshape(c, DN_CONV_DIM)

    g_all, beta_all = _dn_gates(ab_ref[...], alog_ref[...], dtb_ref[...])
    ng = ng_ref[...]
    rowid = lax.broadcasted_iota(jnp.int32, (c, 1), 0)
    local = _dn_local(_dn_items(qkv, g_all, beta_all, 0), lt_ref[...], sext_ref[...], t)
    for h in range(DN_HEADS):
        u, w, attn, q_dec, k_dec, tot = local[h]
        ws, qs = [], []
        for r in range(nreq):
            s = s_ref[r, h]
            ws.append(_mm1(w, s)[r * t:(r + 1) * t, :])
            qs.append(_mm1(q_dec, s)[r * t:(r + 1) * t, :])
        v_new = u - jnp.concatenate(ws, axis=0)
        o = jnp.concatenate(qs, axis=0) + _mm1(attn, v_new)
        for r in range(nreq):
            in_req = jnp.logical_and(rowid >= r * t, rowid < (r + 1) * t)
            kd_r = jnp.where(in_req, k_dec, 0.0)
            s_out_ref[r, h] = (s_ref[r, h] * jnp.exp(tot[r * t:r * t + 1, :])
                               + _mm1(kd_r, v_new, TN))
        o_ref[:, h * dh:(h + 1) * dh] = _dn_out(o, z_ref[:, h * dh:(h + 1) * dh], ng)


def _dn_sample(qkv, z, ab, conv_prev, ssm_prev, conv_w, alog, dtb, ng, t):
    n = qkv.shape[0]
    b = n // t
    c = DN_CHUNK
    assert t == 8 and c % t == 0 and b % (c // t) == 0
    nreq = c // t
    lt, sext = _dn_masks(t)
    const = lambda shape: pl.BlockSpec(shape, lambda i: tuple(0 for _ in shape))
    return pl.pallas_call(
        functools.partial(_dn_sample_kernel, t=t),
        grid=(b // nreq,),
        in_specs=[pl.BlockSpec((nreq, t, DN_CONV_DIM), lambda i: (i, 0, 0)),
                  pl.BlockSpec((c, DN_WIDTH), lambda i: (i, 0)),
                  pl.BlockSpec((c, LANES), lambda i: (i, 0)),
                  pl.BlockSpec((nreq, DN_CONV_W - 1, DN_CONV_DIM), lambda i: (i, 0, 0)),
                  pl.BlockSpec((nreq, DN_HEADS, DN_HEAD_DIM, DN_HEAD_DIM), lambda i: (i, 0, 0, 0)),
                  const((DN_CONV_W, DN_CONV_DIM)), const((1, LANES)), const((1, LANES)),
                  const((1, DN_HEAD_DIM)), const(lt.shape), const(sext.shape)],
        out_specs=[pl.BlockSpec((c, DN_WIDTH), lambda i: (i, 0)),
                   pl.BlockSpec((nreq, DN_HEADS, DN_HEAD_DIM, DN_HEAD_DIM), lambda i: (i, 0, 0, 0)),
                   pl.BlockSpec((nreq, DN_CONV_W - 1, DN_CONV_DIM), lambda i: (i, 0, 0))],
        out_shape=[jax.ShapeDtypeStruct((n, DN_WIDTH), F32),
                   jax.ShapeDtypeStruct(ssm_prev.shape, F32),
                   jax.ShapeDtypeStruct(conv_prev.shape, F32)],
        scratch_shapes=[pltpu.VMEM((nreq, 8 + t, DN_CONV_DIM), F32)],
        compiler_params=_params(("arbitrary",)),
        name="deltanet_sample",
    )(qkv.reshape(b, t, DN_CONV_DIM), z, ab, conv_prev, ssm_prev, conv_w, alog, dtb, ng, lt, sext)


LOG2E = 1.4426950408889634


SOFTPLUS_CLAMP = 100.0


def _sb_softplus(z2, vis):
    sp = jnp.maximum(jnp.log2(1.0 + jnp.exp2(jnp.minimum(z2, SOFTPLUS_CLAMP))), z2)
    if vis is not None:
        sp = jnp.where(vis, sp, 0.0)
    return sp.astype(BF16)


def _sb_weights(z2, cum, carry, vis):
    w = jnp.exp2(z2 + cum + carry)
    if vis is not None:
        w = jnp.where(vis, w, 0.0)
    return w.astype(BF16)


def _neg_upper(tk):
    j = jnp.arange(tk)[:, None]
    s = jnp.arange(tk)[None, :]
    return -(j >= s).astype(BF16)


def _div(x, d):
    if d & (d - 1) == 0:
        return x >> (d.bit_length() - 1)
    return lax.div(x, jnp.int32(d))


SB_CHUNK_PAGES = 8


def _sb_kernel(pt_ref, bias_ref, q_ref, k_ref, v_ref, u_ref, g_ref,
               qs_ref, kn_ref, vn_ref, brow_ref, us_ref, gs_ref, poolk_ref, poolv_ref,
               o_ref, os_ref,
               acc_scr, carry_scr, z0_scr, z1_scr, w0_scr, w1_scr,
               kbuf, vbuf, sacc_scr, scarry_scr, cnt_scr, sem, *, tq, tk, n_pages):
    z_scr = (z0_scr, z1_scr)
    w_scr = (w0_scr, w1_scr)
    p = pl.program_id(0)
    i = pl.program_id(1)
    first_step = jnp.logical_and(p == 0, i == 0)
    last_step = jnp.logical_and(p == pl.num_programs(0) - 1, i == pl.num_programs(1) - 1)
    nsub = tq // tk
    chains = [(hh, r) for hh in range(2) for r in range(nsub)]
    nch = len(chains)
    n_tiles = nsub * (i + 1)
    lane = lax.broadcasted_iota(jnp.int32, (1, LANES), 1)
    head_lanes = [lane < SB_HEAD_DIM, lane >= SB_HEAD_DIM]
    qs = q_ref[...] * (SB_HEAD_DIM ** -0.5 * LOG2E)
    qh = {(hh, r): jnp.where(head_lanes[hh], qs[r * tk:(r + 1) * tk, :], 0.0).astype(BF16)
          for hh, r in chains}
    bias = [bias_ref[2 * p] * LOG2E, bias_ref[2 * p + 1] * LOG2E]
    uneg = u_ref[...]
    col_minus_row = (lax.broadcasted_iota(jnp.int32, (tk, tk), 1)
                     - lax.broadcasted_iota(jnp.int32, (tk, tk), 0))
    acc_scr[...] = jnp.zeros_like(acc_scr)
    carry_scr[...] = jnp.zeros_like(carry_scr)

    n_req, t, _ = qs_ref.shape
    page = kbuf.shape[-1]
    npg = SB_CHUNK_PAGES
    parts = n_pages // npg
    n_chunks = n_req * parts
    rows = SB_HEADS * t
    t_shift = t.bit_length() - 1
    head_shift = SB_HEAD_DIM.bit_length() - 1

    def page_copies(c):
        slot = c & 1
        r = _div(c, parts)
        part = c - r * parts
        copies = []
        for n in range(npg):
            pg = pt_ref[r * n_pages + (n_pages - 1 - (part * npg + n))]
            copies.append(pltpu.make_async_copy(poolk_ref.at[pg], kbuf.at[slot, n], sem.at[0, slot]))
            copies.append(pltpu.make_async_copy(poolv_ref.at[pg], vbuf.at[slot, n], sem.at[1, slot]))
        return copies

    def fetch_next_and_wait(c):
        @pl.when(c + 1 < n_chunks)
        def _():
            for cp in page_copies(c + 1):
                cp.start()

        for cp in page_copies(c):
            cp.wait()

    def sample_chunk(c):
        slot = c & 1
        r = _div(c, parts)
        first = (c - r * parts) == 0
        half = (range(0, 1 + npg // 2), range(1 + npg // 2, 1 + npg))
        st = {"zs": {}, "cums": {}}

        def logits(h):
            brow = brow_ref[...] * LOG2E
            if h == 0:
                row = lax.broadcasted_iota(jnp.int32, (rows, SB_WIDTH), 0)
                col = lax.broadcasted_iota(jnp.int32, (rows, SB_WIDTH), 1)
                q8 = qs_ref[r] * (SB_HEAD_DIM ** -0.5 * LOG2E)
                qt = jnp.concatenate([q8] * SB_HEADS, axis=0)
                st["qbd"] = jnp.where((row >> t_shift) == (col >> head_shift), qt, 0.0).astype(BF16)
            for n in half[h]:
                if n == 0:
                    pad = jnp.zeros((page - t, SB_WIDTH), F32)
                    kn = jnp.concatenate([kn_ref[r], pad], axis=0).astype(BF16)
                    st["zs"][n] = _dot(st["qbd"], kn, NT) + brow
                else:
                    st["zs"][n] = _dot(st["qbd"], kbuf[slot, n - 1].astype(BF16)) + brow

        def sums(h):
            us = us_ref[...]
            for n in half[h]:
                if n == 0:
                    krow = lax.broadcasted_iota(jnp.int32, (rows, page), 0)
                    kcol = lax.broadcasted_iota(jnp.int32, (rows, page), 1)
                    st["vis"] = kcol < jnp.where(first, krow & (t - 1), 0)
                    st["cums"][n] = _dot(_sb_softplus(st["zs"][n], st["vis"]), us)
                else:
                    st["cums"][n] = _dot(_sb_softplus(st["zs"][n], None), us)

        def values(h):
            if h == 0:
                carry = jnp.where(first, 0.0, scarry_scr[...])
                acc = jnp.where(first, 0.0, sacc_scr[...])
            else:
                carry, acc = st["carry"], st["acc"]
            for n in half[h]:
                if n == 0:
                    pad = jnp.zeros((page - t, SB_WIDTH), F32)
                    vn = jnp.concatenate([vn_ref[r], pad], axis=0).astype(BF16)
                    w = _sb_weights(st["zs"][n], st["cums"][n], carry, st["vis"])
                    acc = acc + _dot(w, vn)
                else:
                    w = _sb_weights(st["zs"][n], st["cums"][n], carry, None)
                    acc = acc + _dot(w, vbuf[slot, n - 1].astype(BF16), NT)
                carry = carry + st["cums"][n][:, 0:1]
            st["carry"], st["acc"] = carry, acc
            if h == 1:
                sacc_scr[...] = acc
                scarry_scr[...] = carry

        return [functools.partial(f, h) for f in (logits, sums, values) for h in (0, 1)]

    def sample_output(c):
        acc = sacc_scr[...]
        col = lax.broadcasted_iota(jnp.int32, (t, SB_WIDTH), 1)
        out = jnp.zeros((t, SB_WIDTH), F32)
        for h in range(SB_HEADS):
            out = out + jnp.where((col >> head_shift) == h, acc[h * t:(h + 1) * t, :], 0.0)
        sq = out * out
        ms = jnp.zeros((t, SB_WIDTH), F32)
        for h in range(SB_HEADS):
            in_h = (col >> head_shift) == h
            ms_h = jnp.sum(jnp.where(in_h, sq, 0.0), axis=-1, keepdims=True) * (1.0 / SB_HEAD_DIM)
            ms = jnp.where(in_h, ms_h, ms)
        os_ref[_div(c, parts)] = out * lax.rsqrt(ms + NORM_EPS) * gs_ref[...]

    @pl.when(first_step)
    def _():
        cnt_scr[0] = 0
        sacc_scr[...] = jnp.zeros_like(sacc_scr)
        scarry_scr[...] = jnp.zeros_like(scarry_scr)
        for cp in page_copies(0):
            cp.start()

    def key_tile(n):
        j = jnp.maximum(n_tiles - 1 - n, 0)
        return j, pl.multiple_of(j * tk, tk)

    def logits(n, slot, ci):
        _, start = key_tile(n)
        kt = k_ref[:, pl.ds(start, tk)]
        z_scr[slot][ci] = _dot(qh[chains[ci]], kt) + bias[chains[ci][0]]

    def values(n, slot, ci):
        _, start = key_tile(n)
        vt = v_ref[:, pl.ds(start, tk)]
        acc_scr[ci] += _dot(w_scr[slot][ci], vt, NT)

    def visit_phases(n, slot, ahead, with_values):
        j, _ = key_tile(n)
        live = [ahead is None or ahead <= r for _, r in chains]
        prev_live = [ahead is None or ahead + 1 <= r for _, r in chains]
        vis = [col_minus_row < (i * tq + r * tk - j * tk) if ahead == r else None for _, r in chains]
        zs, sps, cums = [None] * nch, [None] * nch, [None] * nch

        def phase1():
            for ci in range(nch):
                if live[ci]:
                    zs[ci] = z_scr[slot][ci]
                    sps[ci] = _sb_softplus(zs[ci], vis[ci])
                logits(n + 1, 1 - slot, ci)

        def phase2():
            for ci in range(nch):
                if live[ci]:
                    cums[ci] = _dot(sps[ci], uneg)
                if with_values and prev_live[ci]:
                    values(n - 1, 1 - slot, ci)

        def phase3():
            for ci in range(nch):
                if live[ci]:
                    w_scr[slot][ci] = _sb_weights(zs[ci], cums[ci], carry_scr[ci], vis[ci])
                    carry_scr[ci] += cums[ci][:, 0:1]

        return phase1, phase2, phase3

    def visit(n, slot, ahead, with_values):
        for phase in visit_phases(n, slot, ahead, with_values):
            phase()

    for ci in range(nch):
        logits(0, 0, ci)
    for n in range(nsub):
        visit(n, n % 2, nsub - 1 - n, n > 0)

    n_trips = (n_tiles - nsub) // 2
    c0 = cnt_scr[0]
    n_fused = jnp.minimum(n_trips, n_chunks - c0)

    def fused_trip(m, carry):
        c = c0 + m
        fetch_next_and_wait(c)
        l0, l1, s0, s1, v0, v1 = sample_chunk(c)
        n = nsub + 2 * m
        a1, a2, a3 = visit_phases(n, nsub % 2, None, True)
        b1, b2, b3 = visit_phases(n + 1, (nsub + 1) % 2, None, True)
        a1()
        l0()
        l1()
        sample_output(jnp.maximum(c - 1, 0))
        a2()
        s0()
        s1()
        a3()
        b1()
        v0()
        v1()
        b2()
        b3()
        return carry

    def plain_trip(m, carry):
        n = nsub + 2 * m
        visit(n, nsub % 2, None, True)
        visit(n + 1, (nsub + 1) % 2, None, True)
        return carry

    lax.fori_loop(0, n_fused, fused_trip, 0)
    lax.fori_loop(n_fused, n_trips, plain_trip, 0)
    cnt_scr[0] = c0 + n_fused
    for ci in range(nch):
        values(n_tiles - 1, (nsub - 1) % 2, ci)
    for r in range(nsub):
        out = jnp.where(head_lanes[0], acc_scr[chains.index((0, r))], acc_scr[chains.index((1, r))])
        sq = out * out
        ms = [jnp.sum(jnp.where(m, sq, 0.0), axis=-1, keepdims=True) * (1.0 / SB_HEAD_DIM)
              for m in head_lanes]
        ms = jnp.where(head_lanes[0], ms[0], ms[1])
        o_ref[r * tk:(r + 1) * tk, :] = out * lax.rsqrt(ms + NORM_EPS) * g_ref[...]

    @pl.when(last_step)
    def _():
        def rest(c, carry):
            fetch_next_and_wait(c)
            sample_output(jnp.maximum(c - 1, 0))
            for stage in sample_chunk(c):
                stage()
            return carry

        lax.fori_loop(c0 + n_fused, n_chunks, rest, 0)
        sample_output(n_chunks - 1)


def _sb_attention(q, kt_bf, vt_bf, q_s, k_new, v_new, pool_k, pool_v, page_table, bias, g_sb,
                  tq=512, tk=256):
    t_len = q.shape[0]
    tq = min(tq, t_len)
    assert t_len % tq == 0 and tq % (2 * tk) == 0
    b, t, _ = q_s.shape
    n_pages = page_table.shape[1]
    page = pool_k.shape[2]
    npg = SB_CHUNK_PAGES
    assert n_pages % npg == 0 and page == LANES and t == 8
    npairs = SB_WIDTH // LANES
    nchains = 2 * (tq // tk)
    rows = SB_HEADS * t
    brow = jnp.broadcast_to(jnp.repeat(bias, t)[:, None], (rows, LANES))
    g2 = jnp.tile(g_sb, LANES // SB_HEAD_DIM).reshape(1, LANES)
    g8 = jnp.tile(g_sb, SB_HEADS).reshape(1, SB_WIDTH)
    smem = pl.BlockSpec(memory_space=pltpu.SMEM)
    hbm = pl.BlockSpec(memory_space=pl.ANY)
    whole = lambda a: pl.BlockSpec(a.shape, lambda p, i: (0,) * a.ndim, pipeline_mode=pl.Buffered(1))
    return pl.pallas_call(
        functools.partial(_sb_kernel, tq=tq, tk=tk, n_pages=n_pages),
        grid=(npairs, t_len // tq),
        in_specs=[smem, smem,
                  pl.BlockSpec((tq, LANES), lambda p, i: (i, p)),
                  pl.BlockSpec((LANES, t_len), lambda p, i: (p, 0)),
                  pl.BlockSpec((LANES, t_len), lambda p, i: (p, 0)),
                  pl.BlockSpec((tk, tk), lambda p, i: (0, 0)),
                  pl.BlockSpec((1, LANES), lambda p, i: (0, 0)),
                  whole(q_s), whole(k_new), whole(v_new),
                  pl.BlockSpec((rows, LANES), lambda p, i: (0, 0)),
                  pl.BlockSpec((page, page), lambda p, i: (0, 0)),
                  pl.BlockSpec((1, SB_WIDTH), lambda p, i: (0, 0)),
                  hbm, hbm],
        out_specs=[pl.BlockSpec((tq, LANES), lambda p, i: (i, p)),
                   pl.BlockSpec((b, t, SB_WIDTH), lambda p, i: (0, 0, 0))],
        out_shape=[jax.ShapeDtypeStruct((t_len, SB_WIDTH), F32),
                   jax.ShapeDtypeStruct((b, t, SB_WIDTH), F32)],
        scratch_shapes=[pltpu.VMEM((nchains, tk, LANES), F32),
                        pltpu.VMEM((nchains, tk, 1), F32),
                        pltpu.VMEM((nchains, tk, tk), F32), pltpu.VMEM((nchains, tk, tk), F32),
                        pltpu.VMEM((nchains, tk, tk), BF16), pltpu.VMEM((nchains, tk, tk), BF16),
                        pltpu.VMEM((2, npg, SB_WIDTH, page), F32),
                        pltpu.VMEM((2, npg, SB_WIDTH, page), F32),
                        pltpu.VMEM((rows, SB_WIDTH), F32),
                        pltpu.VMEM((rows, 1), F32),
                        pltpu.SMEM((1,), jnp.int32),
                        pltpu.SemaphoreType.DMA((2, 2))],
        compiler_params=_params(("arbitrary", "arbitrary")),
        name="sb_attention",
    )(page_table.reshape(-1), bias, q, kt_bf, vt_bf, _neg_upper(tk), g2,
      q_s, k_new, v_new, brow, _neg_upper(page), g8, pool_k, pool_v)


def _split_w_in(w_in):
    d = w_in.shape[0]
    o_ab = DN_CONV_DIM + DN_WIDTH
    o_q = o_ab + 2 * DN_HEADS
    o_k = o_q + SB_WIDTH
    o_v = o_k + SB_WIDTH
    pad = jnp.zeros((d, LANES - 2 * DN_HEADS), w_in.dtype)
    w_r = jnp.concatenate([w_in[:, 0:o_ab], w_in[:, o_q:o_k], w_in[:, o_ab:o_q], pad], axis=1)
    return w_r.astype(BF16), w_in[:, o_k:o_v].astype(BF16), w_in[:, o_v:].astype(BF16)


def _heads_last(xt, n):
    return jnp.transpose(xt.reshape(SB_HEADS, SB_HEAD_DIM, n), (2, 0, 1))


def _lane_row(v):
    return jnp.pad(v.astype(F32), (0, LANES - v.shape[0])).reshape(1, LANES)


def kernel(x_prompt, x_sample, c_prompt, c_sample, cache_sb_k, cache_sb_v, page_table, state_dn_ssm, state_dn_conv, w_ada, b_ada, g_ffn1, w1_gate, w1_up, w1_down, g_mix, w_in, dn_conv_w, dn_a_log, dn_dt_bias, dn_norm_g, sb_norm_g, sb_bias, w_out, g_ffn2, w2_gate, w2_up, w2_down, g_final):
    depth = w_ada.shape[0]
    bsz, seq, d = x_prompt.shape
    dec_b, dec_t, _ = x_sample.shape
    assert bsz == 1
    xp, xs = x_prompt, x_sample
    c_all = jnp.concatenate([c_prompt, c_sample], axis=0)
    n_c = c_all.shape[0]
    c_all = jnp.pad(c_all, ((0, (-n_c) % 8), (0, 0)))
    outs = {name: [] for name in ("kp", "vp", "ssmp", "convp", "ks", "vs", "ssms", "convs")}
    for l in range(depth):
        last = l == depth - 1
        mods = _adaln(c_all, w_ada[l].astype(BF16), b_ada[l]).reshape(-1, N_MOD, d)
        mp = [mods[0:bsz, i:i + 1, :] for i in range(N_MOD)]
        ms = [mods[bsz:bsz + dec_b, i:i + 1, :] for i in range(N_MOD)]
        w1 = (w1_gate[l].astype(BF16), w1_up[l].astype(BF16), w1_down[l].astype(BF16))
        w2 = (w2_gate[l].astype(BF16), w2_up[l].astype(BF16), w2_down[l].astype(BF16))
        w_in_r, w_k, w_v = _split_w_in(w_in[l])
        w_out_b = w_out[l].astype(BF16)
        alog = _lane_row(dn_a_log[l])
        dtb = _lane_row(dn_dt_bias[l])
        ng = dn_norm_g[l].reshape(1, DN_HEAD_DIM).astype(F32)
        g_sb = sb_norm_g[l].astype(F32)
        bias = sb_bias[l].astype(F32)

        xp = _ffn(xp, mp[0], mp[1], mp[2], g_ffn1[l], *w1)
        qkv, z, q_p, ab, kt, vt, kt_bf, vt_bf = _proj(xp, mp[3], mp[4], g_mix[l], w_in_r, w_k.T, w_v.T, True)
        o_dn_p, ssm_p, conv_p = _dn_prompt(qkv, z, ab, dn_conv_w[l], alog, dtb, ng)
        outs["kp"].append(_heads_last(kt, seq).reshape(bsz, seq, SB_HEADS, SB_HEAD_DIM))
        outs["vp"].append(_heads_last(vt, seq).reshape(bsz, seq, SB_HEADS, SB_HEAD_DIM))
        outs["ssmp"].append(ssm_p.reshape(bsz, DN_HEADS, DN_HEAD_DIM, DN_HEAD_DIM))
        outs["convp"].append(conv_p.reshape(bsz, DN_CONV_W - 1, DN_CONV_DIM))

        xs = _ffn(xs, ms[0], ms[1], ms[2], g_ffn1[l], *w1)
        qkv, z, q, ab, k, v = _proj(xs, ms[3], ms[4], g_mix[l], w_in_r, w_k, w_v, False)
        o_dn, ssm_s, conv_s = _dn_sample(qkv, z, ab, state_dn_conv[l], state_dn_ssm[l],
                                         dn_conv_w[l], alog, dtb, ng, dec_t)
        n_pool, page = cache_sb_k.shape[1], cache_sb_k.shape[2]
        pool_k = jnp.transpose(cache_sb_k[l], (0, 2, 3, 1)).reshape(n_pool, SB_WIDTH, page)
        pool_v = jnp.transpose(cache_sb_v[l], (0, 2, 3, 1)).reshape(n_pool, SB_WIDTH, page)
        o_sb_p, o_sb_s = _sb_attention(q_p, kt_bf, vt_bf, q.reshape(dec_b, dec_t, SB_WIDTH),
                                       k.reshape(dec_b, dec_t, SB_WIDTH), v.reshape(dec_b, dec_t, SB_WIDTH),
                                       pool_k, pool_v, page_table, bias, g_sb)
        xp = _outproj(xp, mp[5], o_dn_p, o_sb_p, w_out_b)
        xp = _ffn(xp, mp[6], mp[7], mp[8], g_ffn2[l], *w2, gfin=g_final if last else None)
        xs = _outproj(xs, ms[5], o_dn, o_sb_s.reshape(dec_b * dec_t, SB_WIDTH), w_out_b)
        xs = _ffn(xs, ms[6], ms[7], ms[8], g_ffn2[l], *w2, gfin=g_final if last else None)
        outs["ks"].append(k.reshape(dec_b, dec_t, SB_HEADS, SB_HEAD_DIM))
        outs["vs"].append(v.reshape(dec_b, dec_t, SB_HEADS, SB_HEAD_DIM))
        outs["ssms"].append(ssm_s)
        outs["convs"].append(conv_s)
    if depth == 0:
        raise ValueError("depth must be positive")
    return (xp, xs,
            jnp.stack(outs["kp"]), jnp.stack(outs["vp"]), jnp.stack(outs["ssmp"]), jnp.stack(outs["convp"]),
            jnp.stack(outs["ks"]), jnp.stack(outs["vs"]), jnp.stack(outs["ssms"]), jnp.stack(outs["convs"]))
```

```python
import functools

import jax
import jax.numpy as jnp
from jax import lax
from jax.experimental import pallas as pl
from jax.experimental.pallas import tpu as pltpu

F32 = jnp.float32
BF16 = jnp.bfloat16

DN_HEADS = 4
DN_HEAD_DIM = 128
DN_WIDTH = DN_HEADS * DN_HEAD_DIM
DN_CONV_W = 4
DN_CONV_DIM = 3 * DN_WIDTH
DN_CHUNK = 64
SB_HEADS = 8
SB_HEAD_DIM = 64
SB_WIDTH = SB_HEADS * SB_HEAD_DIM
N_MOD = 9
NORM_EPS = 1e-6
L2_EPS = 1e-6
LANES = 128

NN = (((1,), (0,)), ((), ()))
NT = (((1,), (1,)), ((), ()))
TN = (((0,), (0,)), ((), ()))

VMEM_LIMIT = 56 * 1024 * 1024


def _params(sem):
    return pltpu.CompilerParams(dimension_semantics=sem, vmem_limit_bytes=VMEM_LIMIT)


def _dot(a, b, dims=NN):
    return lax.dot_general(a, b, dims, preferred_element_type=F32)


def _split2(x):
    hi = x.astype(BF16)
    lo = (x - hi.astype(F32)).astype(BF16)
    return hi, lo


def _split3(x):
    hi = x.astype(BF16)
    r = x - hi.astype(F32)
    mid = r.astype(BF16)
    lo = (r - mid.astype(F32)).astype(BF16)
    return hi, mid, lo


def _mm3(a, b, dims=NN):
    ah, al = _split2(a)
    bh, bl = _split2(b)
    return _dot(ah, bh, dims) + (_dot(ah, bl, dims) + _dot(al, bh, dims))


def _mm1(a, b, dims=NN):
    return _dot(a.astype(BF16), b.astype(BF16), dims)


def _mm_mask_left(m01, x):
    hi, mid, lo = _split3(x)
    return _dot(m01, hi) + (_dot(m01, mid) + _dot(m01, lo))


def _sigmoid(x):
    return 1.0 / (1.0 + jnp.exp(-x))


def _silu(x):
    return x * _sigmoid(x)


def _softplus(x):
    return jnp.maximum(x, 0.0) + jnp.log1p(jnp.exp(-jnp.abs(x)))


def _modnorm(x, gain, shift, scale):
    y = x * lax.rsqrt(jnp.mean(x * x, axis=-1, keepdims=True) + NORM_EPS)
    return (y * gain) * (1.0 + scale) + shift


def _adaln_kernel(c_ref, w_ref, b_ref, o_ref):
    c = _silu(c_ref[...]).astype(BF16)
    o_ref[...] = _dot(c, w_ref[...]) + b_ref[...]


def _adaln(c, w_bf, b):
    n, d = c.shape
    cols = w_bf.shape[1]
    tn = d
    return pl.pallas_call(
        _adaln_kernel,
        grid=(cols // tn,),
        in_specs=[pl.BlockSpec((n, d), lambda j: (0, 0)),
                  pl.BlockSpec((d, tn), lambda j: (0, j)),
                  pl.BlockSpec((1, tn), lambda j: (0, j))],
        out_specs=pl.BlockSpec((n, tn), lambda j: (0, j)),
        out_shape=jax.ShapeDtypeStruct((n, cols), F32),
        compiler_params=_params(("arbitrary",)),
        name="adaln",
    )(c, w_bf, b.reshape(1, cols))


def _row_blocks(g, r, target):
    if r >= target:
        assert r % target == 0
        return 1, target
    gb = min(g, target // r)
    assert g % gb == 0
    return gb, r


MXU_WIDTH = 256


def _ffn_kernel(x_ref, sh_ref, sc_ref, gt_ref, gain_ref, wg_ref, wu_ref, wd_ref, *rest, final_norm):
    if final_norm:
        gfin_ref, o_ref, acc_scr = rest
    else:
        o_ref, acc_scr = rest
    rows, d = acc_scr.shape
    x = x_ref[...]
    h = _modnorm(x, gain_ref[...], sh_ref[...], sc_ref[...]).reshape(rows, d).astype(BF16)
    dff = wg_ref.shape[1]
    for f in range(dff // MXU_WIDTH):
        cols = slice(f * MXU_WIDTH, (f + 1) * MXU_WIDTH)
        a = _dot(h, wg_ref[:, cols])
        b = _dot(h, wu_ref[:, cols])
        part = _dot((_silu(a) * b).astype(BF16), wd_ref[cols, :])
        if f == 0:
            acc_scr[...] = part
        else:
            acc_scr[...] += part
    y = x + 0.5 * gt_ref[...] * acc_scr[...].reshape(x.shape)
    if final_norm:
        y = y * lax.rsqrt(jnp.mean(y * y, axis=-1, keepdims=True) + NORM_EPS) * gfin_ref[...]
    o_ref[...] = y


def _ffn(x, shift, scale, gate, gain, wg, wu, wd, gfin=None, rows_target=512):
    g, r, d = x.shape
    dff = wg.shape[1]
    assert dff % MXU_WIDTH == 0
    gb, rb = _row_blocks(g, r, rows_target)
    rows = gb * rb
    mod_spec = pl.BlockSpec((gb, 1, d), lambda i, j: (i, 0, 0))
    vec_spec = pl.BlockSpec((1, d), lambda i, j: (0, 0))
    resident = lambda a: pl.BlockSpec(a.shape, lambda i, j: (0, 0), pipeline_mode=pl.Buffered(1))
    in_specs = [pl.BlockSpec((gb, rb, d), lambda i, j: (i, j, 0)),
                mod_spec, mod_spec, mod_spec, vec_spec,
                resident(wg), resident(wu), resident(wd)]
    args = [x, shift, scale, gate, gain.reshape(1, d), wg, wu, wd]
    if gfin is not None:
        in_specs.append(vec_spec)
        args.append(gfin.reshape(1, d))
    return pl.pallas_call(
        functools.partial(_ffn_kernel, final_norm=gfin is not None),
        grid=(g // gb, r // rb),
        in_specs=in_specs,
        out_specs=pl.BlockSpec((gb, rb, d), lambda i, j: (i, j, 0)),
        out_shape=jax.ShapeDtypeStruct(x.shape, F32),
        scratch_shapes=[pltpu.VMEM((rows, d), F32)],
        compiler_params=_params(("arbitrary", "arbitrary")),
        name="ffn",
    )(*args)


PROJ_COLS = (DN_CONV_DIM, DN_WIDTH, SB_WIDTH, LANES)


def _proj_kernel(x_ref, sh_ref, sc_ref, gain_ref, w_ref, wk_ref, wv_ref, *o_refs, transposed):
    rows = o_refs[0].shape[0]
    h = _modnorm(x_ref[...], gain_ref[...], sh_ref[...], sc_ref[...])
    h = h.reshape(rows, h.shape[-1]).astype(BF16)
    off = 0
    for o_ref, cols in zip(o_refs[0:len(PROJ_COLS)], PROJ_COLS):
        o_ref[...] = _dot(h, w_ref[:, off:off + cols])
        off += cols
    if transposed:
        k_ref, v_ref, kb_ref, vb_ref = o_refs[len(PROJ_COLS):]
        kt = _dot(wk_ref[...], h, NT)
        vt = _dot(wv_ref[...], h, NT)
        k_ref[...] = kt
        v_ref[...] = vt
        kb_ref[...] = kt.astype(BF16)
        vb_ref[...] = vt.astype(BF16)
    else:
        k_ref, v_ref = o_refs[len(PROJ_COLS):]
        k_ref[...] = _dot(h, wk_ref[...])
        v_ref[...] = _dot(h, wv_ref[...])


def _proj(x, shift, scale, gain, w_r, w_k, w_v, transposed, rows_target=512):
    g, r, d = x.shape
    gb, rb = _row_blocks(g, r, rows_target)
    rows = gb * rb
    nj = r // rb
    n = g * r
    mod_spec = pl.BlockSpec((gb, 1, d), lambda i, j: (i, 0, 0))
    const = lambda a: pl.BlockSpec(a.shape, lambda i, j: (0, 0))
    out_specs = [pl.BlockSpec((rows, c), lambda i, j: (i * nj + j, 0)) for c in PROJ_COLS]
    out_shape = [jax.ShapeDtypeStruct((n, c), F32) for c in PROJ_COLS]
    if transposed:
        out_specs += [pl.BlockSpec((SB_WIDTH, rows), lambda i, j: (0, i * nj + j))] * 4
        out_shape += [jax.ShapeDtypeStruct((SB_WIDTH, n), dt) for dt in (F32, F32, BF16, BF16)]
    else:
        out_specs += [pl.BlockSpec((rows, SB_WIDTH), lambda i, j: (i * nj + j, 0))] * 2
        out_shape += [jax.ShapeDtypeStruct((n, SB_WIDTH), F32)] * 2
    return pl.pallas_call(
        functools.partial(_proj_kernel, transposed=transposed),
        grid=(g // gb, nj),
        in_specs=[pl.BlockSpec((gb, rb, d), lambda i, j: (i, j, 0)),
                  mod_spec, mod_spec,
                  pl.BlockSpec((1, d), lambda i, j: (0, 0)),
                  const(w_r), const(w_k), const(w_v)],
        out_specs=out_specs,
        out_shape=out_shape,
        compiler_params=_params(("arbitrary", "arbitrary")),
        name="proj_in",
    )(x, shift, scale, gain.reshape(1, d), w_r, w_k, w_v)


def _outproj_kernel(x_ref, gt_ref, odn_ref, osb_ref, w_ref, o_ref):
    x = x_ref[...]
    mixed = (_dot(odn_ref[...].astype(BF16), w_ref[0:DN_WIDTH, :])
             + _dot(osb_ref[...].astype(BF16), w_ref[DN_WIDTH:DN_WIDTH + SB_WIDTH, :]))
    o_ref[...] = x + gt_ref[...] * mixed.reshape(x.shape)


def _outproj(x, gate, o_dn, o_sb, w_out, rows_target=512):
    g, r, d = x.shape
    gb, rb = _row_blocks(g, r, rows_target)
    rows = gb * rb
    nj = r // rb
    return pl.pallas_call(
        _outproj_kernel,
        grid=(g // gb, nj),
        in_specs=[pl.BlockSpec((gb, rb, d), lambda i, j: (i, j, 0)),
                  pl.BlockSpec((gb, 1, d), lambda i, j: (i, 0, 0)),
                  pl.BlockSpec((rows, DN_WIDTH), lambda i, j: (i * nj + j, 0)),
                  pl.BlockSpec((rows, SB_WIDTH), lambda i, j: (i * nj + j, 0)),
                  pl.BlockSpec(w_out.shape, lambda i, j: (0, 0))],
        out_specs=pl.BlockSpec((gb, rb, d), lambda i, j: (i, j, 0)),
        out_shape=jax.ShapeDtypeStruct(x.shape, F32),
        compiler_params=_params(("arbitrary", "arbitrary")),
        name="proj_out",
    )(x, gate, o_dn, o_sb, w_out)


def _dn_masks(group):
    c = DN_CHUNK
    i = jnp.arange(c)[:, None]
    m = jnp.arange(c)[None, :]
    same = (i // group) == (m // group)
    lower = jnp.logical_and(m <= i, same)
    lt = jnp.concatenate([lower, same], axis=0).astype(BF16)
    sext = jnp.concatenate([(i > m), jnp.ones((c, c), bool)], axis=1).astype(F32)
    return lt, sext


def _dn_gates(ab, alog, dtb):
    g = -jnp.exp(alog) * _softplus(ab + dtb)
    beta = _sigmoid(ab)
    return g, beta


def _dn_local(items, lt, sext, group):
    c = DN_CHUNK
    row = lax.broadcasted_iota(jnp.int32, (c, c), 0)
    col = lax.broadcasted_iota(jnp.int32, (c, c), 1)
    lower = col <= row
    strict = col < row
    if group < c:
        shift = group.bit_length() - 1
        same = (row >> shift) == (col >> shift)
        lower = jnp.logical_and(lower, same)
        strict = jnp.logical_and(strict, same)
    eye = (row == col).astype(F32)
    n_sq = max(1, (min(group, c) - 1).bit_length() - 1)

    qs, ks, kbs, dms, gams, tots = [], [], [], [], [], []
    for q_raw, k_raw, _, g_col, beta_col in items:
        qs.append(q_raw * lax.rsqrt(jnp.sum(q_raw * q_raw, axis=-1, keepdims=True) + L2_EPS)
                  * (DN_HEAD_DIM ** -0.5))
        k = k_raw * lax.rsqrt(jnp.sum(k_raw * k_raw, axis=-1, keepdims=True) + L2_EPS)
        ks.append(k)
        kbs.append(k * beta_col)
    for _, _, _, g_col, _ in items:
        r = _mm_mask_left(lt, g_col * sext)
        diff = r[0:c, 0:c]
        gams.append(r[0:c, c:c + 1])
        tots.append(r[c:2 * c, c:c + 1])
        dms.append(jnp.where(lower, jnp.exp(jnp.where(lower, diff, 0.0)), 0.0))
    a_s = [jnp.where(strict, _mm1(kb, k, NT) * dm, 0.0) for kb, k, dm in zip(kbs, ks, dms)]
    attns = [_mm1(q, k, NT) * dm for q, k, dm in zip(qs, ks, dms)]
    ps = [eye - a for a in a_s]
    xs = a_s
    for _ in range(n_sq):
        xs = [_mm3(x, x) for x in xs]
        ps = [p + _mm3(p, x) for p, x in zip(ps, xs)]
    egs = [jnp.exp(gam) for gam in gams]
    uws = [_mm1(p, jnp.concatenate([it[2] * it[4], kb * eg], axis=1))
           for p, it, kb, eg in zip(ps, items, kbs, egs)]
    out = []
    for q, k, uw, attn, eg, gam, tot in zip(qs, ks, uws, attns, egs, gams, tots):
        out.append((uw[:, 0:DN_HEAD_DIM], uw[:, DN_HEAD_DIM:2 * DN_HEAD_DIM], attn,
                    q * eg, k * jnp.exp(tot - gam), tot))
    return out


def _dn_items(qkv, g_all, beta_all, r0):
    c = DN_CHUNK
    dh = DN_HEAD_DIM
    items = []
    for h in range(DN_HEADS):
        items.append((qkv[r0:r0 + c, h * dh:(h + 1) * dh],
                      qkv[r0:r0 + c, DN_WIDTH + h * dh:DN_WIDTH + (h + 1) * dh],
                      qkv[r0:r0 + c, 2 * DN_WIDTH + h * dh:2 * DN_WIDTH + (h + 1) * dh],
                      g_all[r0:r0 + c, h:h + 1],
                      beta_all[r0:r0 + c, DN_HEADS + h:DN_HEADS + h + 1]))
    return items


def _dn_out(o, z, ng):
    y = o * lax.rsqrt(jnp.mean(o * o, axis=-1, keepdims=True) + NORM_EPS) * ng
    return y * _silu(z)


def _dn_prompt_kernel(qkv_ref, z_ref, ab_ref, cw_ref, alog_ref, dtb_ref, ng_ref, lt_ref, sext_ref,
                      o_ref, s_out_ref, conv_out_ref, xp_scr, s_scr):
    c = DN_CHUNK
    dh = DN_HEAD_DIM
    tb = qkv_ref.shape[0]
    step = pl.program_id(0)

    @pl.when(step == 0)
    def _():
        xp_scr[0:8, :] = jnp.zeros((8, DN_CONV_DIM), F32)
        s_scr[...] = jnp.zeros_like(s_scr)

    x = qkv_ref[...]
    xp_scr[8:8 + tb, :] = x
    cw = cw_ref[...]
    y = xp_scr[5:5 + tb, :] * cw[0:1, :]
    y = y + xp_scr[6:6 + tb, :] * cw[1:2, :]
    y = y + xp_scr[7:7 + tb, :] * cw[2:3, :]
    y = y + x * cw[3:4, :]
    conv_out_ref[...] = xp_scr[5 + tb:8 + tb, :]
    xp_scr[0:8, :] = xp_scr[tb:tb + 8, :]
    qkv = _silu(y)

    g_all, beta_all = _dn_gates(ab_ref[...], alog_ref[...], dtb_ref[...])
    ng = ng_ref[...]
    items = []
    for n in range(tb // c):
        items += _dn_items(qkv, g_all, beta_all, n * c)
    local = _dn_local(items, lt_ref[...], sext_ref[...], c)
    s = [s_scr[h] for h in range(DN_HEADS)]
    for n in range(tb // c):
        loc = local[n * DN_HEADS:(n + 1) * DN_HEADS]
        v_new = [u - _mm1(w, s[h]) for h, (u, w, _, _, _, _) in enumerate(loc)]
        o = [_mm1(q_dec, s[h]) + _mm1(attn, v_new[h]) for h, (_, _, attn, q_dec, _, _) in enumerate(loc)]
        s = [s[h] * jnp.exp(tot[0:1, :]) + _mm1(k_dec, v_new[h], TN)
             for h, (_, _, _, _, k_dec, tot) in enumerate(loc)]
        for h in range(DN_HEADS):
            o_ref[n * c:(n + 1) * c, h * dh:(h + 1) * dh] = _dn_out(
                o[h], z_ref[n * c:(n + 1) * c, h * dh:(h + 1) * dh], ng)
    for h in range(DN_HEADS):
        s_scr[h] = s[h]
        s_out_ref[h] = s[h]


def _dn_prompt(qkv, z, ab, conv_w, alog, dtb, ng, chunks_per_step=4):
    t = qkv.shape[0]
    c = DN_CHUNK * min(chunks_per_step, t // DN_CHUNK)
    assert t % c == 0
    lt, sext = _dn_masks(DN_CHUNK)
    const = lambda shape: pl.BlockSpec(shape, lambda i: tuple(0 for _ in shape))
    return pl.pallas_call(
        _dn_prompt_kernel,
        grid=(t // c,),
        in_specs=[pl.BlockSpec((c, DN_CONV_DIM), lambda i: (i, 0)),
                  pl.BlockSpec((c, DN_WIDTH), lambda i: (i, 0)),
                  pl.BlockSpec((c, LANES), lambda i: (i, 0)),
                  const((DN_CONV_W, DN_CONV_DIM)), const((1, LANES)), const((1, LANES)),
                  const((1, DN_HEAD_DIM)), const(lt.shape), const(sext.shape)],
        out_specs=[pl.BlockSpec((c, DN_WIDTH), lambda i: (i, 0)),
                   const((DN_HEADS, DN_HEAD_DIM, DN_HEAD_DIM)),
                   const((DN_CONV_W - 1, DN_CONV_DIM))],
        out_shape=[jax.ShapeDtypeStruct((t, DN_WIDTH), F32),
                   jax.ShapeDtypeStruct((DN_HEADS, DN_HEAD_DIM, DN_HEAD_DIM), F32),
                   jax.ShapeDtypeStruct((DN_CONV_W - 1, DN_CONV_DIM), F32)],
        scratch_shapes=[pltpu.VMEM((c + 8, DN_CONV_DIM), F32),
                        pltpu.VMEM((DN_HEADS, DN_HEAD_DIM, DN_HEAD_DIM), F32)],
        compiler_params=_params(("arbitrary",)),
        name="deltanet_prompt",
    )(qkv, z, ab, conv_w, alog, dtb, ng, lt, sext)


def _dn_sample_kernel(qkv_ref, z_ref, ab_ref, cprev_ref, s_ref, cw_ref, alog_ref, dtb_ref, ng_ref,
                      lt_ref, sext_ref, o_ref, s_out_ref, conv_out_ref, xp_scr, *, t):
    c = DN_CHUNK
    dh = DN_HEAD_DIM
    nreq = c // t
    x3 = qkv_ref[...]
    xp_scr[:, 8 - (DN_CONV_W - 1):8, :] = cprev_ref[...]
    xp_scr[:, 8:8 + t, :] = x3
    cw = cw_ref[...]
    y = xp_scr[:, 5:5 + t, :] * cw[0:1, :]
    y = y + xp_scr[:, 6:6 + t, :] * cw[1:2, :]
    y = y + xp_scr[:, 7:7 + t, :] * cw[2:3, :]
    y = y + x3 * cw[3:4, :]
    conv_out_ref[...] = xp_scr[:, 5 + t:8 + t, :]
    qkv = _silu(y).reshape(c, DN_CONV_DIM)

    g_all, beta_all = _dn_gates(ab_ref[...], alog_ref[...], dtb_ref[...])
    ng = ng_ref[...]
    rowid = lax.broadcasted_iota(jnp.int32, (c, 1), 0)
    local = _dn_local(_dn_items(qkv, g_all, beta_all, 0), lt_ref[...], sext_ref[...], t)
    for h in range(DN_HEADS):
        u, w, attn, q_dec, k_dec, tot = local[h]
        ws, qs = [], []
        for r in range(nreq):
            s = s_ref[r, h]
            ws.append(_mm1(w, s)[r * t:(r + 1) * t, :])
            qs.append(_mm1(q_dec, s)[r * t:(r + 1) * t, :])
        v_new = u - jnp.concatenate(ws, axis=0)
        o = jnp.concatenate(qs, axis=0) + _mm1(attn, v_new)
        for r in range(nreq):
            in_req = jnp.logical_and(rowid >= r * t, rowid < (r + 1) * t)
            kd_r = jnp.where(in_req, k_dec, 0.0)
            s_out_ref[r, h] = (s_ref[r, h] * jnp.exp(tot[r * t:r * t + 1, :])
                               + _mm1(kd_r, v_new, TN))
        o_ref[:, h * dh:(h + 1) * dh] = _dn_out(o, z_ref[:, h * dh:(h + 1) * dh], ng)


def _dn_sample(qkv, z, ab, conv_prev, ssm_prev, conv_w, alog, dtb, ng, t):
    n = qkv.shape[0]
    b = n // t
    c = DN_CHUNK
    assert t == 8 and c % t == 0 and b % (c // t) == 0
    nreq = c // t
    lt, sext = _dn_masks(t)
    const = lambda shape: pl.BlockSpec(shape, lambda i: tuple(0 for _ in shape))
    return pl.pallas_call(
        functools.partial(_dn_sample_kernel, t=t),
        grid=(b // nreq,),
        in_specs=[pl.BlockSpec((nreq, t, DN_CONV_DIM), lambda i: (i, 0, 0)),
                  pl.BlockSpec((c, DN_WIDTH), lambda i: (i, 0)),
                  pl.BlockSpec((c, LANES), lambda i: (i, 0)),
                  pl.BlockSpec((nreq, DN_CONV_W - 1, DN_CONV_DIM), lambda i: (i, 0, 0)),
                  pl.BlockSpec((nreq, DN_HEADS, DN_HEAD_DIM, DN_HEAD_DIM), lambda i: (i, 0, 0, 0)),
                  const((DN_CONV_W, DN_CONV_DIM)), const((1, LANES)), const((1, LANES)),
                  const((1, DN_HEAD_DIM)), const(lt.shape), const(sext.shape)],
        out_specs=[pl.BlockSpec((c, DN_WIDTH), lambda i: (i, 0)),
                   pl.BlockSpec((nreq, DN_HEADS, DN_HEAD_DIM, DN_HEAD_DIM), lambda i: (i, 0, 0, 0)),
                   pl.BlockSpec((nreq, DN_CONV_W - 1, DN_CONV_DIM), lambda i: (i, 0, 0))],
        out_shape=[jax.ShapeDtypeStruct((n, DN_WIDTH), F32),
                   jax.ShapeDtypeStruct(ssm_prev.shape, F32),
                   jax.ShapeDtypeStruct(conv_prev.shape, F32)],
        scratch_shapes=[pltpu.VMEM((nreq, 8 + t, DN_CONV_DIM), F32)],
        compiler_params=_params(("arbitrary",)),
        name="deltanet_sample",
    )(qkv.reshape(b, t, DN_CONV_DIM), z, ab, conv_prev, ssm_prev, conv_w, alog, dtb, ng, lt, sext)


LOG2E = 1.4426950408889634


SOFTPLUS_CLAMP = 100.0


def _sb_softplus(z2, vis):
    sp = jnp.maximum(jnp.log2(1.0 + jnp.exp2(jnp.minimum(z2, SOFTPLUS_CLAMP))), z2)
    if vis is not None:
        sp = jnp.where(vis, sp, 0.0)
    return sp.astype(BF16)


def _sb_weights(z2, cum, carry, vis):
    w = jnp.exp2(z2 + cum + carry)
    if vis is not None:
        w = jnp.where(vis, w, 0.0)
    return w.astype(BF16)


def _neg_upper(tk):
    j = jnp.arange(tk)[:, None]
    s = jnp.arange(tk)[None, :]
    return -(j >= s).astype(BF16)


def _div(x, d):
    if d & (d - 1) == 0:
        return x >> (d.bit_length() - 1)
    return lax.div(x, jnp.int32(d))


SB_CHUNK_PAGES = 8


def _sb_kernel(pt_ref, bias_ref, q_ref, k_ref, v_ref, u_ref, g_ref,
               qs_ref, kn_ref, vn_ref, brow_ref, us_ref, gs_ref, poolk_ref, poolv_ref,
               o_ref, os_ref,
               acc_scr, carry_scr, z0_scr, z1_scr, w0_scr, w1_scr,
               kbuf, vbuf, sacc_scr, scarry_scr, cnt_scr, sem, *, tq, tk, n_pages):
    z_scr = (z0_scr, z1_scr)
    w_scr = (w0_scr, w1_scr)
    p = pl.program_id(0)
    i = pl.program_id(1)
    first_step = jnp.logical_and(p == 0, i == 0)
    last_step = jnp.logical_and(p == pl.num_programs(0) - 1, i == pl.num_programs(1) - 1)
    nsub = tq // tk
    chains = [(hh, r) for hh in range(2) for r in range(nsub)]
    nch = len(chains)
    n_tiles = nsub * (i + 1)
    lane = lax.broadcasted_iota(jnp.int32, (1, LANES), 1)
    head_lanes = [lane < SB_HEAD_DIM, lane >= SB_HEAD_DIM]
    qs = q_ref[...] * (SB_HEAD_DIM ** -0.5 * LOG2E)
    qh = {(hh, r): jnp.where(head_lanes[hh], qs[r * tk:(r + 1) * tk, :], 0.0).astype(BF16)
          for hh, r in chains}
    bias = [bias_ref[2 * p] * LOG2E, bias_ref[2 * p + 1] * LOG2E]
    uneg = u_ref[...]
    col_minus_row = (lax.broadcasted_iota(jnp.int32, (tk, tk), 1)
                     - lax.broadcasted_iota(jnp.int32, (tk, tk), 0))
    acc_scr[...] = jnp.zeros_like(acc_scr)
    carry_scr[...] = jnp.zeros_like(carry_scr)

    n_req, t, _ = qs_ref.shape
    page = kbuf.shape[-1]
    npg = SB_CHUNK_PAGES
    parts = n_pages // npg
    n_chunks = n_req * parts
    rows = SB_HEADS * t
    t_shift = t.bit_length() - 1
    head_shift = SB_HEAD_DIM.bit_length() - 1

    def page_copies(c):
        slot = c & 1
        r = _div(c, parts)
        part = c - r * parts
        copies = []
        for n in range(npg):
            pg = pt_ref[r * n_pages + (n_pages - 1 - (part * npg + n))]
            copies.append(pltpu.make_async_copy(poolk_ref.at[pg], kbuf.at[slot, n], sem.at[0, slot]))
            copies.append(pltpu.make_async_copy(poolv_ref.at[pg], vbuf.at[slot, n], sem.at[1, slot]))
        return copies

    def fetch_next_and_wait(c):
        @pl.when(c + 1 < n_chunks)
        def _():
            for cp in page_copies(c + 1):
                cp.start()

        for cp in page_copies(c):
            cp.wait()

    def sample_chunk(c):
        slot = c & 1
        r = _div(c, parts)
        first = (c - r * parts) == 0
        half = (range(0, 1 + npg // 2), range(1 + npg // 2, 1 + npg))
        st = {"zs": {}, "cums": {}}

        def logits(h):
            brow = brow_ref[...] * LOG2E
            if h == 0:
                row = lax.broadcasted_iota(jnp.int32, (rows, SB_WIDTH), 0)
                col = lax.broadcasted_iota(jnp.int32, (rows, SB_WIDTH), 1)
                q8 = qs_ref[r] * (SB_HEAD_DIM ** -0.5 * LOG2E)
                qt = jnp.concatenate([q8] * SB_HEADS, axis=0)
                st["qbd"] = jnp.where((row >> t_shift) == (col >> head_shift), qt, 0.0).astype(BF16)
            for n in half[h]:
                if n == 0:
                    pad = jnp.zeros((page - t, SB_WIDTH), F32)
                    kn = jnp.concatenate([kn_ref[r], pad], axis=0).astype(BF16)
                    st["zs"][n] = _dot(st["qbd"], kn, NT) + brow
                else:
                    st["zs"][n] = _dot(st["qbd"], kbuf[slot, n - 1].astype(BF16)) + brow

        def sums(h):
            us = us_ref[...]
            for n in half[h]:
                if n == 0:
                    krow = lax.broadcasted_iota(jnp.int32, (rows, page), 0)
                    kcol = lax.broadcasted_iota(jnp.int32, (rows, page), 1)
                    st["vis"] = kcol < jnp.where(first, krow & (t - 1), 0)
                    st["cums"][n] = _dot(_sb_softplus(st["zs"][n], st["vis"]), us)
                else:
                    st["cums"][n] = _dot(_sb_softplus(st["zs"][n], None), us)

        def values(h):
            if h == 0:
                carry = jnp.where(first, 0.0, scarry_scr[...])
                acc = jnp.where(first, 0.0, sacc_scr[...])
            else:
                carry, acc = st["carry"], st["acc"]
            for n in half[h]:
                if n == 0:
                    pad = jnp.zeros((page - t, SB_WIDTH), F32)
                    vn = jnp.concatenate([vn_ref[r], pad], axis=0).astype(BF16)
                    w = _sb_weights(st["zs"][n], st["cums"][n], carry, st["vis"])
                    acc = acc + _dot(w, vn)
                else:
                    w = _sb_weights(st["zs"][n], st["cums"][n], carry, None)
                    acc = acc + _dot(w, vbuf[slot, n - 1].astype(BF16), NT)
                carry = carry + st["cums"][n][:, 0:1]
            st["carry"], st["acc"] = carry, acc
            if h == 1:
                sacc_scr[...] = acc
                scarry_scr[...] = carry

        return [functools.partial(f, h) for f in (logits, sums, values) for h in (0, 1)]

    def sample_output(c):
        acc = sacc_scr[...]
        col = lax.broadcasted_iota(jnp.int32, (t, SB_WIDTH), 1)
        out = jnp.zeros((t, SB_WIDTH), F32)
        for h in range(SB_HEADS):
            out = out + jnp.where((col >> head_shift) == h, acc[h * t:(h + 1) * t, :], 0.0)
        sq = out * out
        ms = jnp.zeros((t, SB_WIDTH), F32)
        for h in range(SB_HEADS):
            in_h = (col >> head_shift) == h
            ms_h = jnp.sum(jnp.where(in_h, sq, 0.0), axis=-1, keepdims=True) * (1.0 / SB_HEAD_DIM)
            ms = jnp.where(in_h, ms_h, ms)
        os_ref[_div(c, parts)] = out * lax.rsqrt(ms + NORM_EPS) * gs_ref[...]

    @pl.when(first_step)
    def _():
        cnt_scr[0] = 0
        sacc_scr[...] = jnp.zeros_like(sacc_scr)
        scarry_scr[...] = jnp.zeros_like(scarry_scr)
        for cp in page_copies(0):
            cp.start()

    def key_tile(n):
        j = jnp.maximum(n_tiles - 1 - n, 0)
        return j, pl.multiple_of(j * tk, tk)

    def logits(n, slot, ci):
        _, start = key_tile(n)
        kt = k_ref[:, pl.ds(start, tk)]
        z_scr[slot][ci] = _dot(qh[chains[ci]], kt) + bias[chains[ci][0]]

    def values(n, slot, ci):
        _, start = key_tile(n)
        vt = v_ref[:, pl.ds(start, tk)]
        acc_scr[ci] += _dot(w_scr[slot][ci], vt, NT)

    def visit_phases(n, slot, ahead, with_values):
        j, _ = key_tile(n)
        live = [ahead is None or ahead <= r for _, r in chains]
        prev_live = [ahead is None or ahead + 1 <= r for _, r in chains]
        vis = [col_minus_row < (i * tq + r * tk - j * tk) if ahead == r else None for _, r in chains]
        zs, sps, cums = [None] * nch, [None] * nch, [None] * nch

        def phase1():
            for ci in range(nch):
                if live[ci]:
                    zs[ci] = z_scr[slot][ci]
                    sps[ci] = _sb_softplus(zs[ci], vis[ci])
                logits(n + 1, 1 - slot, ci)

        def phase2():
            for ci in range(nch):
                if live[ci]:
                    cums[ci] = _dot(sps[ci], uneg)
                if with_values and prev_live[ci]:
                    values(n - 1, 1 - slot, ci)

        def phase3():
            for ci in range(nch):
                if live[ci]:
                    w_scr[slot][ci] = _sb_weights(zs[ci], cums[ci], carry_scr[ci], vis[ci])
                    carry_scr[ci] += cums[ci][:, 0:1]

        return phase1, phase2, phase3

    def visit(n, slot, ahead, with_values):
        for phase in visit_phases(n, slot, ahead, with_values):
            phase()

    for ci in range(nch):
        logits(0, 0, ci)
    for n in range(nsub):
        visit(n, n % 2, nsub - 1 - n, n > 0)

    n_trips = (n_tiles - nsub) // 2
    c0 = cnt_scr[0]

    def fused_trip(m, c):
        fetch_next_and_wait(c)
        l0, l1, s0, s1, v0, v1 = sample_chunk(c)
        n = nsub + 2 * m
        a1, a2, a3 = visit_phases(n, nsub % 2, None, True)
        b1, b2, b3 = visit_phases(n + 1, (nsub + 1) % 2, None, True)
        a1()
        l0()
        l1()
        sample_output(jnp.maximum(c - 1, 0))
        a2()
        s0()
        s1()
        a3()
        b1()
        v0()
        v1()
        b2()
        b3()

    def plain_trip(m):
        n = nsub + 2 * m
        visit(n, nsub % 2, None, True)
        visit(n + 1, (nsub + 1) % 2, None, True)

    n_pairs = jnp.minimum(n_trips // 2, n_chunks - c0)

    def pair_of_trips(k, carry):
        fused_trip(2 * k, c0 + k)
        plain_trip(2 * k + 1)
        return carry

    lax.fori_loop(0, n_pairs, pair_of_trips, 0)
    n_single = jnp.minimum(n_trips - 2 * n_pairs, n_chunks - c0 - n_pairs)

    def single_trip(k, carry):
        fused_trip(2 * n_pairs + k, c0 + n_pairs + k)
        return carry

    lax.fori_loop(0, n_single, single_trip, 0)
    n_fused = n_pairs + n_single

    def chunkless_trip(m, carry):
        plain_trip(m)
        return carry

    lax.fori_loop(2 * n_pairs + n_single, n_trips, chunkless_trip, 0)
    cnt_scr[0] = c0 + n_fused
    for ci in range(nch):
        values(n_tiles - 1, (nsub - 1) % 2, ci)
    for r in range(nsub):
        out = jnp.where(head_lanes[0], acc_scr[chains.index((0, r))], acc_scr[chains.index((1, r))])
        sq = out * out
        ms = [jnp.sum(jnp.where(m, sq, 0.0), axis=-1, keepdims=True) * (1.0 / SB_HEAD_DIM)
              for m in head_lanes]
        ms = jnp.where(head_lanes[0], ms[0], ms[1])
        o_ref[r * tk:(r + 1) * tk, :] = out * lax.rsqrt(ms + NORM_EPS) * g_ref[...]

    @pl.when(last_step)
    def _():
        def rest(c, carry):
            fetch_next_and_wait(c)
            sample_output(jnp.maximum(c - 1, 0))
            for stage in sample_chunk(c):
                stage()
            return carry

        lax.fori_loop(c0 + n_fused, n_chunks, rest, 0)
        sample_output(n_chunks - 1)


def _sb_attention(q, kt_bf, vt_bf, q_s, k_new, v_new, pool_k, pool_v, page_table, bias, g_sb,
                  tq=512, tk=256):
    t_len = q.shape[0]
    tq = min(tq, t_len)
    assert t_len % tq == 0 and tq % (2 * tk) == 0
    b, t, _ = q_s.shape
    n_pages = page_table.shape[1]
    page = pool_k.shape[2]
    npg = SB_CHUNK_PAGES
    assert n_pages % npg == 0 and page == LANES and t == 8
    npairs = SB_WIDTH // LANES
    nchains = 2 * (tq // tk)
    rows = SB_HEADS * t
    brow = jnp.broadcast_to(jnp.repeat(bias, t)[:, None], (rows, LANES))
    g2 = jnp.tile(g_sb, LANES // SB_HEAD_DIM).reshape(1, LANES)
    g8 = jnp.tile(g_sb, SB_HEADS).reshape(1, SB_WIDTH)
    smem = pl.BlockSpec(memory_space=pltpu.SMEM)
    hbm = pl.BlockSpec(memory_space=pl.ANY)
    whole = lambda a: pl.BlockSpec(a.shape, lambda p, i: (0,) * a.ndim, pipeline_mode=pl.Buffered(1))
    return pl.pallas_call(
        functools.partial(_sb_kernel, tq=tq, tk=tk, n_pages=n_pages),
        grid=(npairs, t_len // tq),
        in_specs=[smem, smem,
                  pl.BlockSpec((tq, LANES), lambda p, i: (i, p)),
                  pl.BlockSpec((LANES, t_len), lambda p, i: (p, 0)),
                  pl.BlockSpec((LANES, t_len), lambda p, i: (p, 0)),
                  pl.BlockSpec((tk, tk), lambda p, i: (0, 0)),
                  pl.BlockSpec((1, LANES), lambda p, i: (0, 0)),
                  whole(q_s), whole(k_new), whole(v_new),
                  pl.BlockSpec((rows, LANES), lambda p, i: (0, 0)),
                  pl.BlockSpec((page, page), lambda p, i: (0, 0)),
                  pl.BlockSpec((1, SB_WIDTH), lambda p, i: (0, 0)),
                  hbm, hbm],
        out_specs=[pl.BlockSpec((tq, LANES), lambda p, i: (i, p)),
                   pl.BlockSpec((b, t, SB_WIDTH), lambda p, i: (0, 0, 0))],
        out_shape=[jax.ShapeDtypeStruct((t_len, SB_WIDTH), F32),
                   jax.ShapeDtypeStruct((b, t, SB_WIDTH), F32)],
        scratch_shapes=[pltpu.VMEM((nchains, tk, LANES), F32),
                        pltpu.VMEM((nchains, tk, 1), F32),
                        pltpu.VMEM((nchains, tk, tk), F32), pltpu.VMEM((nchains, tk, tk), F32),
                        pltpu.VMEM((nchains, tk, tk), BF16), pltpu.VMEM((nchains, tk, tk), BF16),
                        pltpu.VMEM((2, npg, SB_WIDTH, page), F32),
                        pltpu.VMEM((2, npg, SB_WIDTH, page), F32),
                        pltpu.VMEM((rows, SB_WIDTH), F32),
                        pltpu.VMEM((rows, 1), F32),
                        pltpu.SMEM((1,), jnp.int32),
                        pltpu.SemaphoreType.DMA((2, 2))],
        compiler_params=_params(("arbitrary", "arbitrary")),
        name="sb_attention",
    )(page_table.reshape(-1), bias, q, kt_bf, vt_bf, _neg_upper(tk), g2,
      q_s, k_new, v_new, brow, _neg_upper(page), g8, pool_k, pool_v)


def _split_w_in(w_in):
    d = w_in.shape[0]
    o_ab = DN_CONV_DIM + DN_WIDTH
    o_q = o_ab + 2 * DN_HEADS
    o_k = o_q + SB_WIDTH
    o_v = o_k + SB_WIDTH
    pad = jnp.zeros((d, LANES - 2 * DN_HEADS), w_in.dtype)
    w_r = jnp.concatenate([w_in[:, 0:o_ab], w_in[:, o_q:o_k], w_in[:, o_ab:o_q], pad], axis=1)
    return w_r.astype(BF16), w_in[:, o_k:o_v].astype(BF16), w_in[:, o_v:].astype(BF16)


def _heads_last(xt, n):
    return jnp.transpose(xt.reshape(SB_HEADS, SB_HEAD_DIM, n), (2, 0, 1))


def _lane_row(v):
    return jnp.pad(v.astype(F32), (0, LANES - v.shape[0])).reshape(1, LANES)


def kernel(x_prompt, x_sample, c_prompt, c_sample, cache_sb_k, cache_sb_v, page_table, state_dn_ssm, state_dn_conv, w_ada, b_ada, g_ffn1, w1_gate, w1_up, w1_down, g_mix, w_in, dn_conv_w, dn_a_log, dn_dt_bias, dn_norm_g, sb_norm_g, sb_bias, w_out, g_ffn2, w2_gate, w2_up, w2_down, g_final):
    depth = w_ada.shape[0]
    bsz, seq, d = x_prompt.shape
    dec_b, dec_t, _ = x_sample.shape
    assert bsz == 1
    xp, xs = x_prompt, x_sample
    c_all = jnp.concatenate([c_prompt, c_sample], axis=0)
    n_c = c_all.shape[0]
    c_all = jnp.pad(c_all, ((0, (-n_c) % 8), (0, 0)))
    outs = {name: [] for name in ("kp", "vp", "ssmp", "convp", "ks", "vs", "ssms", "convs")}
    for l in range(depth):
        last = l == depth - 1
        mods = _adaln(c_all, w_ada[l].astype(BF16), b_ada[l]).reshape(-1, N_MOD, d)
        mp = [mods[0:bsz, i:i + 1, :] for i in range(N_MOD)]
        ms = [mods[bsz:bsz + dec_b, i:i + 1, :] for i in range(N_MOD)]
        w1 = (w1_gate[l].astype(BF16), w1_up[l].astype(BF16), w1_down[l].astype(BF16))
        w2 = (w2_gate[l].astype(BF16), w2_up[l].astype(BF16), w2_down[l].astype(BF16))
        w_in_r, w_k, w_v = _split_w_in(w_in[l])
        w_out_b = w_out[l].astype(BF16)
        alog = _lane_row(dn_a_log[l])
        dtb = _lane_row(dn_dt_bias[l])
        ng = dn_norm_g[l].reshape(1, DN_HEAD_DIM).astype(F32)
        g_sb = sb_norm_g[l].astype(F32)
        bias = sb_bias[l].astype(F32)

        xp = _ffn(xp, mp[0], mp[1], mp[2], g_ffn1[l], *w1)
        qkv, z, q_p, ab, kt, vt, kt_bf, vt_bf = _proj(xp, mp[3], mp[4], g_mix[l], w_in_r, w_k.T, w_v.T, True)
        o_dn_p, ssm_p, conv_p = _dn_prompt(qkv, z, ab, dn_conv_w[l], alog, dtb, ng)
        outs["kp"].append(_heads_last(kt, seq).reshape(bsz, seq, SB_HEADS, SB_HEAD_DIM))
        outs["vp"].append(_heads_last(vt, seq).reshape(bsz, seq, SB_HEADS, SB_HEAD_DIM))
        outs["ssmp"].append(ssm_p.reshape(bsz, DN_HEADS, DN_HEAD_DIM, DN_HEAD_DIM))
        outs["convp"].append(conv_p.reshape(bsz, DN_CONV_W - 1, DN_CONV_DIM))

        xs = _ffn(xs, ms[0], ms[1], ms[2], g_ffn1[l], *w1)
        qkv, z, q, ab, k, v = _proj(xs, ms[3], ms[4], g_mix[l], w_in_r, w_k, w_v, False)
        o_dn, ssm_s, conv_s = _dn_sample(qkv, z, ab, state_dn_conv[l], state_dn_ssm[l],
                                         dn_conv_w[l], alog, dtb, ng, dec_t)
        n_pool, page = cache_sb_k.shape[1], cache_sb_k.shape[2]
        pool_k = jnp.transpose(cache_sb_k[l], (0, 2, 3, 1)).reshape(n_pool, SB_WIDTH, page)
        pool_v = jnp.transpose(cache_sb_v[l], (0, 2, 3, 1)).reshape(n_pool, SB_WIDTH, page)
        o_sb_p, o_sb_s = _sb_attention(q_p, kt_bf, vt_bf, q.reshape(dec_b, dec_t, SB_WIDTH),
                                       k.reshape(dec_b, dec_t, SB_WIDTH), v.reshape(dec_b, dec_t, SB_WIDTH),
                                       pool_k, pool_v, page_table, bias, g_sb)
        xp = _outproj(xp, mp[5], o_dn_p, o_sb_p, w_out_b)
        xp = _ffn(xp, mp[6], mp[7], mp[8], g_ffn2[l], *w2, gfin=g_final if last else None)
        xs = _outproj(xs, ms[5], o_dn, o_sb_s.reshape(dec_b * dec_t, SB_WIDTH), w_out_b)
        xs = _ffn(xs, ms[6], ms[7], ms[8], g_ffn2[l], *w2, gfin=g_final if last else None)
        outs["ks"].append(k.reshape(dec_b, dec_t, SB_HEADS, SB_HEAD_DIM))
        outs["vs"].append(v.reshape(dec_b, dec_t, SB_HEADS, SB_HEAD_DIM))
        outs["ssms"].append(ssm_s)
        outs["convs"].append(conv_s)
    if depth == 0:
        raise ValueError("depth must be positive")
    return (xp, xs,
            jnp.stack(outs["kp"]), jnp.stack(outs["vp"]), jnp.stack(outs["ssmp"]), jnp.stack(outs["convp"]),
            jnp.stack(outs["ks"]), jnp.stack(outs["vs"]), jnp.stack(outs["ssms"]), jnp.stack(outs["convs"]))
```

```python
import functools

import jax
import jax.numpy as jnp
from jax import lax
from jax.experimental import pallas as pl
from jax.experimental.pallas import tpu as pltpu

F32 = jnp.float32
BF16 = jnp.bfloat16

DN_HEADS = 4
DN_HEAD_DIM = 128
DN_WIDTH = DN_HEADS * DN_HEAD_DIM
DN_CONV_W = 4
DN_CONV_DIM = 3 * DN_WIDTH
DN_CHUNK = 64
SB_HEADS = 8
SB_HEAD_DIM = 64
SB_WIDTH = SB_HEADS * SB_HEAD_DIM
N_MOD = 9
NORM_EPS = 1e-6
L2_EPS = 1e-6
LANES = 128

NN = (((1,), (0,)), ((), ()))
NT = (((1,), (1,)), ((), ()))
TN = (((0,), (0,)), ((), ()))

VMEM_LIMIT = 56 * 1024 * 1024


def _params(sem):
    return pltpu.CompilerParams(dimension_semantics=sem, vmem_limit_bytes=VMEM_LIMIT)


def _dot(a, b, dims=NN):
    return lax.dot_general(a, b, dims, preferred_element_type=F32)


def _split2(x):
    hi = x.astype(BF16)
    lo = (x - hi.astype(F32)).astype(BF16)
    return hi, lo


def _split3(x):
    hi = x.astype(BF16)
    r = x - hi.astype(F32)
    mid = r.astype(BF16)
    lo = (r - mid.astype(F32)).astype(BF16)
    return hi, mid, lo


def _mm3(a, b, dims=NN):
    ah, al = _split2(a)
    bh, bl = _split2(b)
    return _dot(ah, bh, dims) + (_dot(ah, bl, dims) + _dot(al, bh, dims))


def _mm1(a, b, dims=NN):
    return _dot(a.astype(BF16), b.astype(BF16), dims)


def _mm_mask_left(m01, x):
    hi, mid, lo = _split3(x)
    return _dot(m01, hi) + (_dot(m01, mid) + _dot(m01, lo))


def _sigmoid(x):
    return 1.0 / (1.0 + jnp.exp(-x))


def _silu(x):
    return x * _sigmoid(x)


def _softplus(x):
    return jnp.maximum(x, 0.0) + jnp.log1p(jnp.exp(-jnp.abs(x)))


def _modnorm(x, gain, shift, scale):
    y = x * lax.rsqrt(jnp.mean(x * x, axis=-1, keepdims=True) + NORM_EPS)
    return (y * gain) * (1.0 + scale) + shift


def _adaln_kernel(c_ref, w_ref, b_ref, o_ref):
    c = _silu(c_ref[...]).astype(BF16)
    o_ref[...] = _dot(c, w_ref[...]) + b_ref[...]


def _adaln(c, w_bf, b):
    n, d = c.shape
    cols = w_bf.shape[1]
    tn = d
    return pl.pallas_call(
        _adaln_kernel,
        grid=(cols // tn,),
        in_specs=[pl.BlockSpec((n, d), lambda j: (0, 0)),
                  pl.BlockSpec((d, tn), lambda j: (0, j)),
                  pl.BlockSpec((1, tn), lambda j: (0, j))],
        out_specs=pl.BlockSpec((n, tn), lambda j: (0, j)),
        out_shape=jax.ShapeDtypeStruct((n, cols), F32),
        compiler_params=_params(("arbitrary",)),
        name="adaln",
    )(c, w_bf, b.reshape(1, cols))


def _row_blocks(g, r, target):
    if r >= target:
        assert r % target == 0
        return 1, target
    gb = min(g, target // r)
    assert g % gb == 0
    return gb, r


MXU_WIDTH = 256


def _ffn_kernel(x_ref, sh_ref, sc_ref, gt_ref, gain_ref, wg_ref, wu_ref, wd_ref, *rest, with_mix, final_norm):
    if with_mix:
        gm_ref, odn_ref, osb_ref, wo_ref = rest[0:4]
        rest = rest[4:]
    if final_norm:
        gfin_ref, o_ref, acc_scr = rest
    else:
        o_ref, acc_scr = rest
    rows, d = acc_scr.shape
    x = x_ref[...]
    if with_mix:
        mixed = (_dot(odn_ref[...].astype(BF16), wo_ref[0:DN_WIDTH, :])
                 + _dot(osb_ref[...].astype(BF16), wo_ref[DN_WIDTH:DN_WIDTH + SB_WIDTH, :]))
        x = x + gm_ref[...] * mixed.reshape(x.shape)
    h = _modnorm(x, gain_ref[...], sh_ref[...], sc_ref[...]).reshape(rows, d).astype(BF16)
    dff = wg_ref.shape[1]
    for f in range(dff // MXU_WIDTH):
        cols = slice(f * MXU_WIDTH, (f + 1) * MXU_WIDTH)
        a = _dot(h, wg_ref[:, cols])
        b = _dot(h, wu_ref[:, cols])
        part = _dot((_silu(a) * b).astype(BF16), wd_ref[cols, :])
        if f == 0:
            acc_scr[...] = part
        else:
            acc_scr[...] += part
    y = x + 0.5 * gt_ref[...] * acc_scr[...].reshape(x.shape)
    if final_norm:
        y = y * lax.rsqrt(jnp.mean(y * y, axis=-1, keepdims=True) + NORM_EPS) * gfin_ref[...]
    o_ref[...] = y


def _ffn(x, shift, scale, gate, gain, wg, wu, wd, mix=None, gfin=None, rows_target=512):
    g, r, d = x.shape
    dff = wg.shape[1]
    assert dff % MXU_WIDTH == 0
    gb, rb = _row_blocks(g, r, rows_target)
    rows = gb * rb
    nj = r // rb
    mod_spec = pl.BlockSpec((gb, 1, d), lambda i, j: (i, 0, 0))
    vec_spec = pl.BlockSpec((1, d), lambda i, j: (0, 0))
    resident = lambda a: pl.BlockSpec(a.shape, lambda i, j: (0, 0), pipeline_mode=pl.Buffered(1))
    in_specs = [pl.BlockSpec((gb, rb, d), lambda i, j: (i, j, 0)),
                mod_spec, mod_spec, mod_spec, vec_spec,
                resident(wg), resident(wu), resident(wd)]
    args = [x, shift, scale, gate, gain.reshape(1, d), wg, wu, wd]
    if mix is not None:
        gate_mix, o_dn, o_sb, w_out = mix
        in_specs += [mod_spec,
                     pl.BlockSpec((rows, o_dn.shape[1]), lambda i, j: (i * nj + j, 0)),
                     pl.BlockSpec((rows, o_sb.shape[1]), lambda i, j: (i * nj + j, 0)),
                     resident(w_out)]
        args += [gate_mix, o_dn, o_sb, w_out]
    if gfin is not None:
        in_specs.append(vec_spec)
        args.append(gfin.reshape(1, d))
    return pl.pallas_call(
        functools.partial(_ffn_kernel, with_mix=mix is not None, final_norm=gfin is not None),
        grid=(g // gb, r // rb),
        in_specs=in_specs,
        out_specs=pl.BlockSpec((gb, rb, d), lambda i, j: (i, j, 0)),
        out_shape=jax.ShapeDtypeStruct(x.shape, F32),
        scratch_shapes=[pltpu.VMEM((rows, d), F32)],
        compiler_params=_params(("arbitrary", "arbitrary")),
        name="ffn",
    )(*args)


PROJ_COLS = (DN_CONV_DIM, DN_WIDTH, SB_WIDTH, LANES)


def _proj_kernel(x_ref, sh_ref, sc_ref, gain_ref, w_ref, wk_ref, wv_ref, *o_refs, transposed):
    rows = o_refs[0].shape[0]
    h = _modnorm(x_ref[...], gain_ref[...], sh_ref[...], sc_ref[...])
    h = h.reshape(rows, h.shape[-1]).astype(BF16)
    off = 0
    for o_ref, cols in zip(o_refs[0:len(PROJ_COLS)], PROJ_COLS):
        o_ref[...] = _dot(h, w_ref[:, off:off + cols])
        off += cols
    if transposed:
        k_ref, v_ref, kb_ref, vb_ref = o_refs[len(PROJ_COLS):]
        kt = _dot(wk_ref[...], h, NT)
        vt = _dot(wv_ref[...], h, NT)
        k_ref[...] = kt
        v_ref[...] = vt
        kb_ref[...] = kt.astype(BF16)
        vb_ref[...] = vt.astype(BF16)
    else:
        k_ref, v_ref = o_refs[len(PROJ_COLS):]
        k_ref[...] = _dot(h, wk_ref[...])
        v_ref[...] = _dot(h, wv_ref[...])


def _proj(x, shift, scale, gain, w_r, w_k, w_v, transposed, rows_target=512):
    g, r, d = x.shape
    gb, rb = _row_blocks(g, r, rows_target)
    rows = gb * rb
    nj = r // rb
    n = g * r
    mod_spec = pl.BlockSpec((gb, 1, d), lambda i, j: (i, 0, 0))
    const = lambda a: pl.BlockSpec(a.shape, lambda i, j: (0, 0))
    out_specs = [pl.BlockSpec((rows, c), lambda i, j: (i * nj + j, 0)) for c in PROJ_COLS]
    out_shape = [jax.ShapeDtypeStruct((n, c), F32) for c in PROJ_COLS]
    if transposed:
        out_specs += [pl.BlockSpec((SB_WIDTH, rows), lambda i, j: (0, i * nj + j))] * 4
        out_shape += [jax.ShapeDtypeStruct((SB_WIDTH, n), dt) for dt in (F32, F32, BF16, BF16)]
    else:
        out_specs += [pl.BlockSpec((rows, SB_WIDTH), lambda i, j: (i * nj + j, 0))] * 2
        out_shape += [jax.ShapeDtypeStruct((n, SB_WIDTH), F32)] * 2
    return pl.pallas_call(
        functools.partial(_proj_kernel, transposed=transposed),
        grid=(g // gb, nj),
        in_specs=[pl.BlockSpec((gb, rb, d), lambda i, j: (i, j, 0)),
                  mod_spec, mod_spec,
                  pl.BlockSpec((1, d), lambda i, j: (0, 0)),
                  const(w_r), const(w_k), const(w_v)],
        out_specs=out_specs,
        out_shape=out_shape,
        compiler_params=_params(("arbitrary", "arbitrary")),
        name="proj_in",
    )(x, shift, scale, gain.reshape(1, d), w_r, w_k, w_v)


def _dn_masks(group):
    c = DN_CHUNK
    i = jnp.arange(c)[:, None]
    m = jnp.arange(c)[None, :]
    same = (i // group) == (m // group)
    lower = jnp.logical_and(m <= i, same)
    lt = jnp.concatenate([lower, same], axis=0).astype(BF16)
    sext = jnp.concatenate([(i > m), jnp.ones((c, c), bool)], axis=1).astype(F32)
    return lt, sext


def _dn_gates(ab, alog, dtb):
    g = -jnp.exp(alog) * _softplus(ab + dtb)
    beta = _sigmoid(ab)
    return g, beta


def _dn_local(items, lt, sext, group):
    c = DN_CHUNK
    row = lax.broadcasted_iota(jnp.int32, (c, c), 0)
    col = lax.broadcasted_iota(jnp.int32, (c, c), 1)
    lower = col <= row
    strict = col < row
    if group < c:
        shift = group.bit_length() - 1
        same = (row >> shift) == (col >> shift)
        lower = jnp.logical_and(lower, same)
        strict = jnp.logical_and(strict, same)
    eye = (row == col).astype(F32)
    n_sq = max(1, (min(group, c) - 1).bit_length() - 1)

    qs, ks, kbs, dms, gams, tots = [], [], [], [], [], []
    for q_raw, k_raw, _, g_col, beta_col in items:
        qs.append(q_raw * lax.rsqrt(jnp.sum(q_raw * q_raw, axis=-1, keepdims=True) + L2_EPS)
                  * (DN_HEAD_DIM ** -0.5))
        k = k_raw * lax.rsqrt(jnp.sum(k_raw * k_raw, axis=-1, keepdims=True) + L2_EPS)
        ks.append(k)
        kbs.append(k * beta_col)
    for _, _, _, g_col, _ in items:
        r = _mm_mask_left(lt, g_col * sext)
        diff = r[0:c, 0:c]
        gams.append(r[0:c, c:c + 1])
        tots.append(r[c:2 * c, c:c + 1])
        dms.append(jnp.where(lower, jnp.exp(jnp.where(lower, diff, 0.0)), 0.0))
    a_s = [jnp.where(strict, _mm1(kb, k, NT) * dm, 0.0) for kb, k, dm in zip(kbs, ks, dms)]
    attns = [_mm1(q, k, NT) * dm for q, k, dm in zip(qs, ks, dms)]
    ps = [eye - a for a in a_s]
    xs = a_s
    for _ in range(n_sq):
        xs = [_mm3(x, x) for x in xs]
        ps = [p + _mm3(p, x) for p, x in zip(ps, xs)]
    egs = [jnp.exp(gam) for gam in gams]
    uws = [_mm1(p, jnp.concatenate([it[2] * it[4], kb * eg], axis=1))
           for p, it, kb, eg in zip(ps, items, kbs, egs)]
    out = []
    for q, k, uw, attn, eg, gam, tot in zip(qs, ks, uws, attns, egs, gams, tots):
        out.append((uw[:, 0:DN_HEAD_DIM], uw[:, DN_HEAD_DIM:2 * DN_HEAD_DIM], attn,
                    q * eg, k * jnp.exp(tot - gam), tot))
    return out


def _dn_items(qkv, g_all, beta_all, r0):
    c = DN_CHUNK
    dh = DN_HEAD_DIM
    items = []
    for h in range(DN_HEADS):
        items.append((qkv[r0:r0 + c, h * dh:(h + 1) * dh],
                      qkv[r0:r0 + c, DN_WIDTH + h * dh:DN_WIDTH + (h + 1) * dh],
                      qkv[r0:r0 + c, 2 * DN_WIDTH + h * dh:2 * DN_WIDTH + (h + 1) * dh],
                      g_all[r0:r0 + c, h:h + 1],
                      beta_all[r0:r0 + c, DN_HEADS + h:DN_HEADS + h + 1]))
    return items


def _dn_out(o, z, ng):
    y = o * lax.rsqrt(jnp.mean(o * o, axis=-1, keepdims=True) + NORM_EPS) * ng
    return y * _silu(z)


def _dn_prompt_kernel(qkv_ref, z_ref, ab_ref, cw_ref, alog_ref, dtb_ref, ng_ref, lt_ref, sext_ref,
                      o_ref, s_out_ref, conv_out_ref, xp_scr, s_scr):
    c = DN_CHUNK
    dh = DN_HEAD_DIM
    tb = qkv_ref.shape[0]
    step = pl.program_id(0)

    @pl.when(step == 0)
    def _():
        xp_scr[0:8, :] = jnp.zeros((8, DN_CONV_DIM), F32)
        s_scr[...] = jnp.zeros_like(s_scr)

    x = qkv_ref[...]
    xp_scr[8:8 + tb, :] = x
    cw = cw_ref[...]
    y = xp_scr[5:5 + tb, :] * cw[0:1, :]
    y = y + xp_scr[6:6 + tb, :] * cw[1:2, :]
    y = y + xp_scr[7:7 + tb, :] * cw[2:3, :]
    y = y + x * cw[3:4, :]
    conv_out_ref[...] = xp_scr[5 + tb:8 + tb, :]
    xp_scr[0:8, :] = xp_scr[tb:tb + 8, :]
    qkv = _silu(y)

    g_all, beta_all = _dn_gates(ab_ref[...], alog_ref[...], dtb_ref[...])
    ng = ng_ref[...]
    items = []
    for n in range(tb // c):
        items += _dn_items(qkv, g_all, beta_all, n * c)
    local = _dn_local(items, lt_ref[...], sext_ref[...], c)
    s = [s_scr[h] for h in range(DN_HEADS)]
    for n in range(tb // c):
        loc = local[n * DN_HEADS:(n + 1) * DN_HEADS]
        v_new = [u - _mm1(w, s[h]) for h, (u, w, _, _, _, _) in enumerate(loc)]
        o = [_mm1(q_dec, s[h]) + _mm1(attn, v_new[h]) for h, (_, _, attn, q_dec, _, _) in enumerate(loc)]
        s = [s[h] * jnp.exp(tot[0:1, :]) + _mm1(k_dec, v_new[h], TN)
             for h, (_, _, _, _, k_dec, tot) in enumerate(loc)]
        for h in range(DN_HEADS):
            o_ref[n * c:(n + 1) * c, h * dh:(h + 1) * dh] = _dn_out(
                o[h], z_ref[n * c:(n + 1) * c, h * dh:(h + 1) * dh], ng)
    for h in range(DN_HEADS):
        s_scr[h] = s[h]
        s_out_ref[h] = s[h]


def _dn_prompt(qkv, z, ab, conv_w, alog, dtb, ng, chunks_per_step=4):
    t = qkv.shape[0]
    c = DN_CHUNK * min(chunks_per_step, t // DN_CHUNK)
    assert t % c == 0
    lt, sext = _dn_masks(DN_CHUNK)
    const = lambda shape: pl.BlockSpec(shape, lambda i: tuple(0 for _ in shape))
    return pl.pallas_call(
        _dn_prompt_kernel,
        grid=(t // c,),
        in_specs=[pl.BlockSpec((c, DN_CONV_DIM), lambda i: (i, 0)),
                  pl.BlockSpec((c, DN_WIDTH), lambda i: (i, 0)),
                  pl.BlockSpec((c, LANES), lambda i: (i, 0)),
                  const((DN_CONV_W, DN_CONV_DIM)), const((1, LANES)), const((1, LANES)),
                  const((1, DN_HEAD_DIM)), const(lt.shape), const(sext.shape)],
        out_specs=[pl.BlockSpec((c, DN_WIDTH), lambda i: (i, 0)),
                   const((DN_HEADS, DN_HEAD_DIM, DN_HEAD_DIM)),
                   const((DN_CONV_W - 1, DN_CONV_DIM))],
        out_shape=[jax.ShapeDtypeStruct((t, DN_WIDTH), F32),
                   jax.ShapeDtypeStruct((DN_HEADS, DN_HEAD_DIM, DN_HEAD_DIM), F32),
                   jax.ShapeDtypeStruct((DN_CONV_W - 1, DN_CONV_DIM), F32)],
        scratch_shapes=[pltpu.VMEM((c + 8, DN_CONV_DIM), F32),
                        pltpu.VMEM((DN_HEADS, DN_HEAD_DIM, DN_HEAD_DIM), F32)],
        compiler_params=_params(("arbitrary",)),
        name="deltanet_prompt",
    )(qkv, z, ab, conv_w, alog, dtb, ng, lt, sext)


def _dn_sample_kernel(qkv_ref, z_ref, ab_ref, cprev_ref, s_ref, cw_ref, alog_ref, dtb_ref, ng_ref,
                      lt_ref, sext_ref, o_ref, s_out_ref, conv_out_ref, xp_scr, *, t):
    c = DN_CHUNK
    dh = DN_HEAD_DIM
    nreq = c // t
    x3 = qkv_ref[...]
    xp_scr[:, 8 - (DN_CONV_W - 1):8, :] = cprev_ref[...]
    xp_scr[:, 8:8 + t, :] = x3
    cw = cw_ref[...]
    y = xp_scr[:, 5:5 + t, :] * cw[0:1, :]
    y = y + xp_scr[:, 6:6 + t, :] * cw[1:2, :]
    y = y + xp_scr[:, 7:7 + t, :] * cw[2:3, :]
    y = y + x3 * cw[3:4, :]
    conv_out_ref[...] = xp_scr[:, 5 + t:8 + t, :]
    qkv = _silu(y).reshape(c, DN_CONV_DIM)

    g_all, beta_all = _dn_gates(ab_ref[...], alog_ref[...], dtb_ref[...])
    ng = ng_ref[...]
    rowid = lax.broadcasted_iota(jnp.int32, (c, 1), 0)
    local = _dn_local(_dn_items(qkv, g_all, beta_all, 0), lt_ref[...], sext_ref[...], t)
    for h in range(DN_HEADS):
        u, w, attn, q_dec, k_dec, tot = local[h]
        ws, qs = [], []
        for r in range(nreq):
            s = s_ref[r, h]
            ws.append(_mm1(w, s)[r * t:(r + 1) * t, :])
            qs.append(_mm1(q_dec, s)[r * t:(r + 1) * t, :])
        v_new = u - jnp.concatenate(ws, axis=0)
        o = jnp.concatenate(qs, axis=0) + _mm1(attn, v_new)
        for r in range(nreq):
            in_req = jnp.logical_and(rowid >= r * t, rowid < (r + 1) * t)
            kd_r = jnp.where(in_req, k_dec, 0.0)
            s_out_ref[r, h] = (s_ref[r, h] * jnp.exp(tot[r * t:r * t + 1, :])
                               + _mm1(kd_r, v_new, TN))
        o_ref[:, h * dh:(h + 1) * dh] = _dn_out(o, z_ref[:, h * dh:(h + 1) * dh], ng)


def _dn_sample(qkv, z, ab, conv_prev, ssm_prev, conv_w, alog, dtb, ng, t):
    n = qkv.shape[0]
    b = n // t
    c = DN_CHUNK
    assert t == 8 and c % t == 0 and b % (c // t) == 0
    nreq = c // t
    lt, sext = _dn_masks(t)
    const = lambda shape: pl.BlockSpec(shape, lambda i: tuple(0 for _ in shape))
    return pl.pallas_call(
        functools.partial(_dn_sample_kernel, t=t),
        grid=(b // nreq,),
        in_specs=[pl.BlockSpec((nreq, t, DN_CONV_DIM), lambda i: (i, 0, 0)),
                  pl.BlockSpec((c, DN_WIDTH), lambda i: (i, 0)),
                  pl.BlockSpec((c, LANES), lambda i: (i, 0)),
                  pl.BlockSpec((nreq, DN_CONV_W - 1, DN_CONV_DIM), lambda i: (i, 0, 0)),
                  pl.BlockSpec((nreq, DN_HEADS, DN_HEAD_DIM, DN_HEAD_DIM), lambda i: (i, 0, 0, 0)),
                  const((DN_CONV_W, DN_CONV_DIM)), const((1, LANES)), const((1, LANES)),
                  const((1, DN_HEAD_DIM)), const(lt.shape), const(sext.shape)],
        out_specs=[pl.BlockSpec((c, DN_WIDTH), lambda i: (i, 0)),
                   pl.BlockSpec((nreq, DN_HEADS, DN_HEAD_DIM, DN_HEAD_DIM), lambda i: (i, 0, 0, 0)),
                   pl.BlockSpec((nreq, DN_CONV_W - 1, DN_CONV_DIM), lambda i: (i, 0, 0))],
        out_shape=[jax.ShapeDtypeStruct((n, DN_WIDTH), F32),
                   jax.ShapeDtypeStruct(ssm_prev.shape, F32),
                   jax.ShapeDtypeStruct(conv_prev.shape, F32)],
        scratch_shapes=[pltpu.VMEM((nreq, 8 + t, DN_CONV_DIM), F32)],
        compiler_params=_params(("arbitrary",)),
        name="deltanet_sample",
    )(qkv.reshape(b, t, DN_CONV_DIM), z, ab, conv_prev, ssm_prev, conv_w, alog, dtb, ng, lt, sext)


LOG2E = 1.4426950408889634


SOFTPLUS_CLAMP = 100.0


def _sb_softplus(z2, vis):
    sp = jnp.maximum(jnp.log2(1.0 + jnp.exp2(jnp.minimum(z2, SOFTPLUS_CLAMP))), z2)
    if vis is not None:
        sp = jnp.where(vis, sp, 0.0)
    return sp.astype(BF16)


def _sb_weights(z2, cum, carry, vis):
    w = jnp.exp2(z2 + cum + carry)
    if vis is not None:
        w = jnp.where(vis, w, 0.0)
    return w.astype(BF16)


def _neg_upper(tk):
    j = jnp.arange(tk)[:, None]
    s = jnp.arange(tk)[None, :]
    return -(j >= s).astype(BF16)


def _div(x, d):
    if d & (d - 1) == 0:
        return x >> (d.bit_length() - 1)
    return lax.div(x, jnp.int32(d))


SB_CHUNK_PAGES = 8


def _sb_kernel(pt_ref, bias_ref, q_ref, k_ref, v_ref, u_ref, g_ref,
               qs_ref, kn_ref, vn_ref, brow_ref, us_ref, gs_ref, poolk_ref, poolv_ref,
               o_ref, os_ref,
               acc_scr, carry_scr, z0_scr, z1_scr, w0_scr, w1_scr,
               kbuf, vbuf, sacc_scr, scarry_scr, cnt_scr, sem, *, tq, tk, n_pages):
    z_scr = (z0_scr, z1_scr)
    w_scr = (w0_scr, w1_scr)
    p = pl.program_id(0)
    i = pl.program_id(1)
    first_step = jnp.logical_and(p == 0, i == 0)
    last_step = jnp.logical_and(p == pl.num_programs(0) - 1, i == pl.num_programs(1) - 1)
    nsub = tq // tk
    chains = [(hh, r) for hh in range(2) for r in range(nsub)]
    nch = len(chains)
    n_tiles = nsub * (i + 1)
    lane = lax.broadcasted_iota(jnp.int32, (1, LANES), 1)
    head_lanes = [lane < SB_HEAD_DIM, lane >= SB_HEAD_DIM]
    qs = q_ref[...] * (SB_HEAD_DIM ** -0.5 * LOG2E)
    qh = {(hh, r): jnp.where(head_lanes[hh], qs[r * tk:(r + 1) * tk, :], 0.0).astype(BF16)
          for hh, r in chains}
    bias = [bias_ref[2 * p] * LOG2E, bias_ref[2 * p + 1] * LOG2E]
    uneg = u_ref[...]
    col_minus_row = (lax.broadcasted_iota(jnp.int32, (tk, tk), 1)
                     - lax.broadcasted_iota(jnp.int32, (tk, tk), 0))
    acc_scr[...] = jnp.zeros_like(acc_scr)
    carry_scr[...] = jnp.zeros_like(carry_scr)

    n_req, t, _ = qs_ref.shape
    page = kbuf.shape[-1]
    npg = SB_CHUNK_PAGES
    parts = n_pages // npg
    n_chunks = n_req * parts
    rows = SB_HEADS * t
    t_shift = t.bit_length() - 1
    head_shift = SB_HEAD_DIM.bit_length() - 1

    def page_copies(c):
        slot = c & 1
        r = _div(c, parts)
        part = c - r * parts
        copies = []
        for n in range(npg):
            pg = pt_ref[r * n_pages + (n_pages - 1 - (part * npg + n))]
            copies.append(pltpu.make_async_copy(poolk_ref.at[pg], kbuf.at[slot, n], sem.at[0, slot]))
            copies.append(pltpu.make_async_copy(poolv_ref.at[pg], vbuf.at[slot, n], sem.at[1, slot]))
        return copies

    def fetch_next_and_wait(c):
        @pl.when(c + 1 < n_chunks)
        def _():
            for cp in page_copies(c + 1):
                cp.start()

        for cp in page_copies(c):
            cp.wait()

    def sample_chunk(c):
        slot = c & 1
        r = _div(c, parts)
        first = (c - r * parts) == 0
        half = (range(0, 1 + npg // 2), range(1 + npg // 2, 1 + npg))
        st = {"zs": {}, "cums": {}}

        def logits(h):
            brow = brow_ref[...] * LOG2E
            if h == 0:
                row = lax.broadcasted_iota(jnp.int32, (rows, SB_WIDTH), 0)
                col = lax.broadcasted_iota(jnp.int32, (rows, SB_WIDTH), 1)
                q8 = qs_ref[r] * (SB_HEAD_DIM ** -0.5 * LOG2E)
                qt = jnp.concatenate([q8] * SB_HEADS, axis=0)
                st["qbd"] = jnp.where((row >> t_shift) == (col >> head_shift), qt, 0.0).astype(BF16)
            for n in half[h]:
                if n == 0:
                    pad = jnp.zeros((page - t, SB_WIDTH), F32)
                    kn = jnp.concatenate([kn_ref[r], pad], axis=0).astype(BF16)
                    st["zs"][n] = _dot(st["qbd"], kn, NT) + brow
                else:
                    st["zs"][n] = _dot(st["qbd"], kbuf[slot, n - 1].astype(BF16)) + brow

        def sums(h):
            us = us_ref[...]
            for n in half[h]:
                if n == 0:
                    krow = lax.broadcasted_iota(jnp.int32, (rows, page), 0)
                    kcol = lax.broadcasted_iota(jnp.int32, (rows, page), 1)
                    st["vis"] = kcol < jnp.where(first, krow & (t - 1), 0)
                    st["cums"][n] = _dot(_sb_softplus(st["zs"][n], st["vis"]), us)
                else:
                    st["cums"][n] = _dot(_sb_softplus(st["zs"][n], None), us)

        def values(h):
            if h == 0:
                carry = jnp.where(first, 0.0, scarry_scr[...])
                acc = jnp.where(first, 0.0, sacc_scr[...])
            else:
                carry, acc = st["carry"], st["acc"]
            for n in half[h]:
                if n == 0:
                    pad = jnp.zeros((page - t, SB_WIDTH), F32)
                    vn = jnp.concatenate([vn_ref[r], pad], axis=0).astype(BF16)
                    w = _sb_weights(st["zs"][n], st["cums"][n], carry, st["vis"])
                    acc = acc + _dot(w, vn)
                else:
                    w = _sb_weights(st["zs"][n], st["cums"][n], carry, None)
                    acc = acc + _dot(w, vbuf[slot, n - 1].astype(BF16), NT)
                carry = carry + st["cums"][n][:, 0:1]
            st["carry"], st["acc"] = carry, acc
            if h == 1:
                sacc_scr[...] = acc
                scarry_scr[...] = carry

        return [functools.partial(f, h) for f in (logits, sums, values) for h in (0, 1)]

    def sample_output(c):
        acc = sacc_scr[...]
        col = lax.broadcasted_iota(jnp.int32, (t, SB_WIDTH), 1)
        out = jnp.zeros((t, SB_WIDTH), F32)
        for h in range(SB_HEADS):
            out = out + jnp.where((col >> head_shift) == h, acc[h * t:(h + 1) * t, :], 0.0)
        sq = out * out
        ms = jnp.zeros((t, SB_WIDTH), F32)
        for h in range(SB_HEADS):
            in_h = (col >> head_shift) == h
            ms_h = jnp.sum(jnp.where(in_h, sq, 0.0), axis=-1, keepdims=True) * (1.0 / SB_HEAD_DIM)
            ms = jnp.where(in_h, ms_h, ms)
        os_ref[_div(c, parts)] = out * lax.rsqrt(ms + NORM_EPS) * gs_ref[...]

    @pl.when(first_step)
    def _():
        cnt_scr[0] = 0
        sacc_scr[...] = jnp.zeros_like(sacc_scr)
        scarry_scr[...] = jnp.zeros_like(scarry_scr)
        for cp in page_copies(0):
            cp.start()

    def key_tile(n):
        j = jnp.maximum(n_tiles - 1 - n, 0)
        return j, pl.multiple_of(j * tk, tk)

    def logits(n, slot, ci):
        _, start = key_tile(n)
        kt = k_ref[:, pl.ds(start, tk)]
        z_scr[slot][ci] = _dot(qh[chains[ci]], kt) + bias[chains[ci][0]]

    def values(n, slot, ci):
        _, start = key_tile(n)
        vt = v_ref[:, pl.ds(start, tk)]
        acc_scr[ci] += _dot(w_scr[slot][ci], vt, NT)

    def visit_phases(n, slot, ahead, with_values):
        j, _ = key_tile(n)
        live = [ahead is None or ahead <= r for _, r in chains]
        prev_live = [ahead is None or ahead + 1 <= r for _, r in chains]
        vis = [col_minus_row < (i * tq + r * tk - j * tk) if ahead == r else None for _, r in chains]
        zs, sps, cums = [None] * nch, [None] * nch, [None] * nch

        def phase1():
            for ci in range(nch):
                if live[ci]:
                    zs[ci] = z_scr[slot][ci]
                    sps[ci] = _sb_softplus(zs[ci], vis[ci])
                logits(n + 1, 1 - slot, ci)

        def phase2():
            for ci in range(nch):
                if live[ci]:
                    cums[ci] = _dot(sps[ci], uneg)
                if with_values and prev_live[ci]:
                    values(n - 1, 1 - slot, ci)

        def phase3():
            for ci in range(nch):
                if live[ci]:
                    w_scr[slot][ci] = _sb_weights(zs[ci], cums[ci], carry_scr[ci], vis[ci])
                    carry_scr[ci] += cums[ci][:, 0:1]

        return phase1, phase2, phase3

    def visit(n, slot, ahead, with_values):
        for phase in visit_phases(n, slot, ahead, with_values):
            phase()

    for ci in range(nch):
        logits(0, 0, ci)
    for n in range(nsub):
        visit(n, n % 2, nsub - 1 - n, n > 0)

    n_trips = (n_tiles - nsub) // 2
    c0 = cnt_scr[0]

    def fused_trip(m, c):
        fetch_next_and_wait(c)
        l0, l1, s0, s1, v0, v1 = sample_chunk(c)
        n = nsub + 2 * m
        a1, a2, a3 = visit_phases(n, nsub % 2, None, True)
        b1, b2, b3 = visit_phases(n + 1, (nsub + 1) % 2, None, True)
        a1()
        l0()
        l1()
        sample_output(jnp.maximum(c - 1, 0))
        a2()
        s0()
        s1()
        a3()
        b1()
        v0()
        v1()
        b2()
        b3()

    def plain_trip(m):
        n = nsub + 2 * m
        visit(n, nsub % 2, None, True)
        visit(n + 1, (nsub + 1) % 2, None, True)

    n_pairs = jnp.minimum(n_trips // 2, n_chunks - c0)

    def pair_of_trips(k, carry):
        fused_trip(2 * k, c0 + k)
        plain_trip(2 * k + 1)
        return carry

    lax.fori_loop(0, n_pairs, pair_of_trips, 0)
    n_single = jnp.minimum(n_trips - 2 * n_pairs, n_chunks - c0 - n_pairs)

    def single_trip(k, carry):
        fused_trip(2 * n_pairs + k, c0 + n_pairs + k)
        return carry

    lax.fori_loop(0, n_single, single_trip, 0)
    n_fused = n_pairs + n_single

    def chunkless_trip(m, carry):
        plain_trip(m)
        return carry

    lax.fori_loop(2 * n_pairs + n_single, n_trips, chunkless_trip, 0)
    cnt_scr[0] = c0 + n_fused
    for ci in range(nch):
        values(n_tiles - 1, (nsub - 1) % 2, ci)
    for r in range(nsub):
        out = jnp.where(head_lanes[0], acc_scr[chains.index((0, r))], acc_scr[chains.index((1, r))])
        sq = out * out
        ms = [jnp.sum(jnp.where(m, sq, 0.0), axis=-1, keepdims=True) * (1.0 / SB_HEAD_DIM)
              for m in head_lanes]
        ms = jnp.where(head_lanes[0], ms[0], ms[1])
        o_ref[r * tk:(r + 1) * tk, :] = out * lax.rsqrt(ms + NORM_EPS) * g_ref[...]

    @pl.when(last_step)
    def _():
        def rest(c, carry):
            fetch_next_and_wait(c)
            sample_output(jnp.maximum(c - 1, 0))
            for stage in sample_chunk(c):
                stage()
            return carry

        lax.fori_loop(c0 + n_fused, n_chunks, rest, 0)
        sample_output(n_chunks - 1)


def _sb_attention(q, kt_bf, vt_bf, q_s, k_new, v_new, pool_k, pool_v, page_table, bias, g_sb,
                  tq=512, tk=256):
    t_len = q.shape[0]
    tq = min(tq, t_len)
    assert t_len % tq == 0 and tq % (2 * tk) == 0
    b, t, _ = q_s.shape
    n_pages = page_table.shape[1]
    page = pool_k.shape[2]
    npg = SB_CHUNK_PAGES
    assert n_pages % npg == 0 and page == LANES and t == 8
    npairs = SB_WIDTH // LANES
    nchains = 2 * (tq // tk)
    rows = SB_HEADS * t
    brow = jnp.broadcast_to(jnp.repeat(bias, t)[:, None], (rows, LANES))
    g2 = jnp.tile(g_sb, LANES // SB_HEAD_DIM).reshape(1, LANES)
    g8 = jnp.tile(g_sb, SB_HEADS).reshape(1, SB_WIDTH)
    smem = pl.BlockSpec(memory_space=pltpu.SMEM)
    hbm = pl.BlockSpec(memory_space=pl.ANY)
    whole = lambda a: pl.BlockSpec(a.shape, lambda p, i: (0,) * a.ndim, pipeline_mode=pl.Buffered(1))
    return pl.pallas_call(
        functools.partial(_sb_kernel, tq=tq, tk=tk, n_pages=n_pages),
        grid=(npairs, t_len // tq),
        in_specs=[smem, smem,
                  pl.BlockSpec((tq, LANES), lambda p, i: (i, p)),
                  pl.BlockSpec((LANES, t_len), lambda p, i: (p, 0)),
                  pl.BlockSpec((LANES, t_len), lambda p, i: (p, 0)),
                  pl.BlockSpec((tk, tk), lambda p, i: (0, 0)),
                  pl.BlockSpec((1, LANES), lambda p, i: (0, 0)),
                  whole(q_s), whole(k_new), whole(v_new),
                  pl.BlockSpec((rows, LANES), lambda p, i: (0, 0)),
                  pl.BlockSpec((page, page), lambda p, i: (0, 0)),
                  pl.BlockSpec((1, SB_WIDTH), lambda p, i: (0, 0)),
                  hbm, hbm],
        out_specs=[pl.BlockSpec((tq, LANES), lambda p, i: (i, p)),
                   pl.BlockSpec((b, t, SB_WIDTH), lambda p, i: (0, 0, 0))],
        out_shape=[jax.ShapeDtypeStruct((t_len, SB_WIDTH), F32),
                   jax.ShapeDtypeStruct((b, t, SB_WIDTH), F32)],
        scratch_shapes=[pltpu.VMEM((nchains, tk, LANES), F32),
                        pltpu.VMEM((nchains, tk, 1), F32),
                        pltpu.VMEM((nchains, tk, tk), F32), pltpu.VMEM((nchains, tk, tk), F32),
                        pltpu.VMEM((nchains, tk, tk), BF16), pltpu.VMEM((nchains, tk, tk), BF16),
                        pltpu.VMEM((2, npg, SB_WIDTH, page), F32),
                        pltpu.VMEM((2, npg, SB_WIDTH, page), F32),
                        pltpu.VMEM((rows, SB_WIDTH), F32),
                        pltpu.VMEM((rows, 1), F32),
                        pltpu.SMEM((1,), jnp.int32),
                        pltpu.SemaphoreType.DMA((2, 2))],
        compiler_params=_params(("arbitrary", "arbitrary")),
        name="sb_attention",
    )(page_table.reshape(-1), bias, q, kt_bf, vt_bf, _neg_upper(tk), g2,
      q_s, k_new, v_new, brow, _neg_upper(page), g8, pool_k, pool_v)


def _split_w_in(w_in):
    d = w_in.shape[0]
    o_ab = DN_CONV_DIM + DN_WIDTH
    o_q = o_ab + 2 * DN_HEADS
    o_k = o_q + SB_WIDTH
    o_v = o_k + SB_WIDTH
    pad = jnp.zeros((d, LANES - 2 * DN_HEADS), w_in.dtype)
    w_r = jnp.concatenate([w_in[:, 0:o_ab], w_in[:, o_q:o_k], w_in[:, o_ab:o_q], pad], axis=1)
    return w_r.astype(BF16), w_in[:, o_k:o_v].astype(BF16), w_in[:, o_v:].astype(BF16)


def _heads_last(xt, n):
    return jnp.transpose(xt.reshape(SB_HEADS, SB_HEAD_DIM, n), (2, 0, 1))


def _lane_row(v):
    return jnp.pad(v.astype(F32), (0, LANES - v.shape[0])).reshape(1, LANES)


def kernel(x_prompt, x_sample, c_prompt, c_sample, cache_sb_k, cache_sb_v, page_table, state_dn_ssm, state_dn_conv, w_ada, b_ada, g_ffn1, w1_gate, w1_up, w1_down, g_mix, w_in, dn_conv_w, dn_a_log, dn_dt_bias, dn_norm_g, sb_norm_g, sb_bias, w_out, g_ffn2, w2_gate, w2_up, w2_down, g_final):
    depth = w_ada.shape[0]
    bsz, seq, d = x_prompt.shape
    dec_b, dec_t, _ = x_sample.shape
    assert bsz == 1
    xp, xs = x_prompt, x_sample
    c_all = jnp.concatenate([c_prompt, c_sample], axis=0)
    n_c = c_all.shape[0]
    c_all = jnp.pad(c_all, ((0, (-n_c) % 8), (0, 0)))
    outs = {name: [] for name in ("kp", "vp", "ssmp", "convp", "ks", "vs", "ssms", "convs")}
    for l in range(depth):
        last = l == depth - 1
        mods = _adaln(c_all, w_ada[l].astype(BF16), b_ada[l]).reshape(-1, N_MOD, d)
        mp = [mods[0:bsz, i:i + 1, :] for i in range(N_MOD)]
        ms = [mods[bsz:bsz + dec_b, i:i + 1, :] for i in range(N_MOD)]
        w1 = (w1_gate[l].astype(BF16), w1_up[l].astype(BF16), w1_down[l].astype(BF16))
        w2 = (w2_gate[l].astype(BF16), w2_up[l].astype(BF16), w2_down[l].astype(BF16))
        w_in_r, w_k, w_v = _split_w_in(w_in[l])
        w_out_b = w_out[l].astype(BF16)
        alog = _lane_row(dn_a_log[l])
        dtb = _lane_row(dn_dt_bias[l])
        ng = dn_norm_g[l].reshape(1, DN_HEAD_DIM).astype(F32)
        g_sb = sb_norm_g[l].astype(F32)
        bias = sb_bias[l].astype(F32)

        xp = _ffn(xp, mp[0], mp[1], mp[2], g_ffn1[l], *w1)
        qkv, z, q_p, ab, kt, vt, kt_bf, vt_bf = _proj(xp, mp[3], mp[4], g_mix[l], w_in_r, w_k.T, w_v.T, True)
        o_dn_p, ssm_p, conv_p = _dn_prompt(qkv, z, ab, dn_conv_w[l], alog, dtb, ng)
        outs["kp"].append(_heads_last(kt, seq).reshape(bsz, seq, SB_HEADS, SB_HEAD_DIM))
        outs["vp"].append(_heads_last(vt, seq).reshape(bsz, seq, SB_HEADS, SB_HEAD_DIM))
        outs["ssmp"].append(ssm_p.reshape(bsz, DN_HEADS, DN_HEAD_DIM, DN_HEAD_DIM))
        outs["convp"].append(conv_p.reshape(bsz, DN_CONV_W - 1, DN_CONV_DIM))

        xs = _ffn(xs, ms[0], ms[1], ms[2], g_ffn1[l], *w1)
        qkv, z, q, ab, k, v = _proj(xs, ms[3], ms[4], g_mix[l], w_in_r, w_k, w_v, False)
        o_dn, ssm_s, conv_s = _dn_sample(qkv, z, ab, state_dn_conv[l], state_dn_ssm[l],
                                         dn_conv_w[l], alog, dtb, ng, dec_t)
        n_pool, page = cache_sb_k.shape[1], cache_sb_k.shape[2]
        pool_k = jnp.transpose(cache_sb_k[l], (0, 2, 3, 1)).reshape(n_pool, SB_WIDTH, page)
        pool_v = jnp.transpose(cache_sb_v[l], (0, 2, 3, 1)).reshape(n_pool, SB_WIDTH, page)
        o_sb_p, o_sb_s = _sb_attention(q_p, kt_bf, vt_bf, q.reshape(dec_b, dec_t, SB_WIDTH),
                                       k.reshape(dec_b, dec_t, SB_WIDTH), v.reshape(dec_b, dec_t, SB_WIDTH),
                                       pool_k, pool_v, page_table, bias, g_sb)
        xp = _ffn(xp, mp[6], mp[7], mp[8], g_ffn2[l], *w2, mix=(mp[5], o_dn_p, o_sb_p, w_out_b),
                  gfin=g_final if last else None)
        xs = _ffn(xs, ms[6], ms[7], ms[8], g_ffn2[l], *w2,
                  mix=(ms[5], o_dn, o_sb_s.reshape(dec_b * dec_t, SB_WIDTH), w_out_b),
                  gfin=g_final if last else None)
        outs["ks"].append(k.reshape(dec_b, dec_t, SB_HEADS, SB_HEAD_DIM))
        outs["vs"].append(v.reshape(dec_b, dec_t, SB_HEADS, SB_HEAD_DIM))
        outs["ssms"].append(ssm_s)
        outs["convs"].append(conv_s)
    if depth == 0:
        raise ValueError("depth must be positive")
    return (xp, xs,
            jnp.stack(outs["kp"]), jnp.stack(outs["vp"]), jnp.stack(outs["ssmp"]), jnp.stack(outs["convp"]),
            jnp.stack(outs["ks"]), jnp.stack(outs["vs"]), jnp.stack(outs["ssms"]), jnp.stack(outs["convs"]))
```
